```python
import jax, jax.numpy as jnp
from jax import lax
import numpy as np

D_MODEL = 1024
BATCH = 4
SEQ = 4096
DEPTH = 1

GRID_W = 64
CTX_LEN = 256
NA_HEADS = 8
NA_HEAD_DIM = 64
NA_WIDTH = NA_HEADS * NA_HEAD_DIM
NA_WIN_ROWS = 8
NA_WIN_COLS = 16
LRU_WIDTH = D_MODEL
LRU_BLOCKS = 8
LRU_BLOCK = LRU_WIDTH // LRU_BLOCKS
CONV_WIDTH = 4
CONV_LEFT = 2
LRU_C = 8.0
PEER_HEADS = 8
PEER_NKEYS = 128
PEER_EXPERTS = PEER_NKEYS * PEER_NKEYS
PEER_QDIM = 256
PEER_HALF = PEER_QDIM // 2
PEER_TOPK = 16
PEER_CHUNK = 128
EPS = 1e-6
NEG_INF = -1e30

Q0 = 0
K0 = Q0 + NA_WIDTH
V0 = K0 + NA_WIDTH
XR0 = V0 + NA_WIDTH
YR0 = XR0 + LRU_WIDTH
GA0 = YR0 + LRU_WIDTH
GB0 = GA0 + D_MODEL
IN_COLS = GB0 + D_MODEL

kernel_name = "hybrid_na_rglru_peer_dit_block"


def rms_norm(x, w):
    xf = x.astype(jnp.float32)
    y = xf * lax.rsqrt(jnp.mean(xf * xf, axis=-1, keepdims=True) + EPS)
    return (y * w.astype(jnp.float32)).astype(x.dtype)


def modulate(h, shift, scale):
    return h * (1.0 + scale[:, None, :]) + shift[:, None, :]


def dwconv_centred(u, w, b):
    T = u.shape[1]
    up = jnp.pad(u, ((0, 0), (CONV_LEFT, CONV_WIDTH - 1 - CONV_LEFT), (0, 0)))
    out = b
    for j in range(CONV_WIDTH):
        out = out + w[j] * up[:, j:j + T]
    return out


def rglru_coeffs(u, w_a, b_a, w_x, b_x, lam):
    B, T, W = u.shape
    uf = u.astype(jnp.float32)
    ub = uf.reshape(B, T, LRU_BLOCKS, LRU_BLOCK)
    r = jax.nn.sigmoid(jnp.einsum('btnc,ncd->btnd', ub, w_a.astype(jnp.float32)).reshape(B, T, W) + b_a)
    i = jax.nn.sigmoid(jnp.einsum('btnc,ncd->btnd', ub, w_x.astype(jnp.float32)).reshape(B, T, W) + b_x)
    log_a = -LRU_C * r * jax.nn.softplus(-lam.astype(jnp.float32))
    a = jnp.exp(log_a)
    beta = jnp.sqrt(-jnp.expm1(2.0 * log_a))
    return a, beta * (i * uf)


def linear_scan(a, b, reverse):
    def combine(left, right):
        a_l, b_l = left
        a_r, b_r = right
        return a_l * a_r, a_r * b_l + b_r
    _, h = lax.associative_scan(combine, (a, b), reverse=reverse, axis=1)
    return h


def rglru_bidirectional(u_lat, u_ctx, w_a, b_a, w_x, b_x, lam):
    out = None
    for d, rev in enumerate((False, True)):
        a_c, b_c = rglru_coeffs(u_ctx, w_a[d], b_a[d], w_x[d], b_x[d], lam[d])
        h_c = linear_scan(a_c, b_c, rev)
        a_l, b_l = rglru_coeffs(u_lat, w_a[d], b_a[d], w_x[d], b_x[d], lam[d])
        if rev:
            b_l = b_l.at[:, -1].add(a_l[:, -1] * h_c[:, 0])
        else:
            b_l = b_l.at[:, 0].add(a_l[:, 0] * h_c[:, -1])
        h_l = linear_scan(a_l, b_l, rev)
        out = h_l if out is None else out + h_l
    return out


def neighbourhood_attention(q, k, v, k_ctx, v_ctx, rpb):
    B, S = q.shape[0], q.shape[1]
    rows = S // GRID_W
    wr = min(NA_WIN_ROWS, rows)
    wc = NA_WIN_COLS
    grid = lambda t: t.reshape(B, rows, GRID_W, NA_HEADS, NA_HEAD_DIM)
    qg, kg, vg = grid(q), grid(k), grid(v)
    r = jnp.arange(rows)
    row0 = jnp.clip(r - wr // 2, 0, rows - wr)
    key_rows = row0[:, None] + jnp.arange(wr)
    k_strip = kg[:, key_rows]
    v_strip = vg[:, key_rows]
    col = jnp.arange(GRID_W)
    col0 = jnp.clip(col - wc // 2, 0, GRID_W - wc)
    col_mask = (col[None, :] >= col0[:, None]) & (col[None, :] < col0[:, None] + wc)
    dr = key_rows - r[:, None]
    dc = jnp.clip(col[None, :] - col[:, None], -(wc - 1), wc - 1)
    bias = rpb[:, dr[:, None, :, None] + NA_WIN_ROWS - 1,
               dc[None, :, None, :] + NA_WIN_COLS - 1]
    scale = NA_HEAD_DIM ** -0.5
    s_win = jnp.einsum('brqhd,brwkhd->bhrqwk', qg, k_strip).astype(jnp.float32) * scale
    s_win = jnp.where(col_mask[:, None, :], s_win + bias.astype(jnp.float32), NEG_INF)
    s_ctx = jnp.einsum('brqhd,bchd->bhrqc', qg, k_ctx).astype(jnp.float32) * scale
    n_win = wr * GRID_W
    s = jnp.concatenate([s_win.reshape(B, NA_HEADS, rows, GRID_W, n_win), s_ctx], axis=-1)
    p = jax.nn.softmax(s, axis=-1).astype(v.dtype)
    p_win = p[..., :n_win].reshape(B, NA_HEADS, rows, GRID_W, wr, GRID_W)
    p_ctx = p[..., n_win:]
    o = (jnp.einsum('bhrqwk,brwkhd->brqhd', p_win, v_strip)
         + jnp.einsum('bhrqc,bchd->brqhd', p_ctx, v_ctx))
    return o.reshape(B, S, NA_WIDTH)


def peer(h, w_q, keys, u, v):
    B, S, D = h.shape
    q = (h @ w_q).reshape(B, S, PEER_HEADS, 2, PEER_HALF)
    s = jnp.einsum('bshpd,hpkd->bshpk', q, keys).astype(jnp.float32)
    top_s, top_i = lax.top_k(s, PEER_TOPK)
    cand_s = (top_s[..., 0, :, None] + top_s[..., 1, None, :]).reshape(B, S, PEER_HEADS, PEER_TOPK * PEER_TOPK)
    cand_i = (top_i[..., 0, :, None] * PEER_NKEYS + top_i[..., 1, None, :]).reshape(B, S, PEER_HEADS, PEER_TOPK * PEER_TOPK)
    best_s, pos = lax.top_k(cand_s, PEER_TOPK)
    expert = jnp.take_along_axis(cand_i, pos, axis=-1)
    gate = jax.nn.softmax(best_s, axis=-1).astype(h.dtype)
    n_chunks = (B * S) // PEER_CHUNK

    def chunk(args):
        hc, ec, gc = args
        act = jax.nn.gelu(jnp.einsum('td,thkd->thk', hc, u[ec]))
        return jnp.einsum('thk,thkd->td', gc * act, v[ec])

    y = lax.map(chunk, (h.reshape(n_chunks, PEER_CHUNK, D),
                        expert.reshape(n_chunks, PEER_CHUNK, PEER_HEADS, PEER_TOPK),
                        gate.reshape(n_chunks, PEER_CHUNK, PEER_HEADS, PEER_TOPK)))
    return y.reshape(B, S, D)


def setup_inputs(seed: int = 0) -> dict:
    key = jax.random.key(seed)
    ks = jax.random.split(key, 26)
    nrm = lambda k, shape, s: jax.random.normal(k, shape, jnp.float32) * s
    a0 = jax.random.uniform(ks[18], (DEPTH, 2, LRU_WIDTH), jnp.float32, 0.9, 0.999)
    p0 = a0 ** (1.0 / LRU_C)
    return {
        "x": nrm(ks[0], (BATCH, SEQ, D_MODEL), 1.0),
        "c": nrm(ks[1], (BATCH, D_MODEL), 1.0),
        "ctx": nrm(ks[2], (BATCH, CTX_LEN, D_MODEL), 1.0),
        "c_ctx": nrm(ks[3], (D_MODEL,), 1.0),
        "w_ada": nrm(ks[4], (DEPTH, D_MODEL, 6 * D_MODEL), 0.5 * D_MODEL ** -0.5),
        "b_ada": nrm(ks[5], (DEPTH, 6 * D_MODEL), 0.02),
        "norm1_w": 1.0 + nrm(ks[6], (DEPTH, D_MODEL), 0.02),
        "norm2_w": 1.0 + nrm(ks[7], (DEPTH, D_MODEL), 0.02),
        "w_in": nrm(ks[8], (DEPTH, D_MODEL, IN_COLS), D_MODEL ** -0.5),
        "q_norm_w": 1.0 + nrm(ks[9], (DEPTH, NA_HEAD_DIM), 0.02),
        "k_norm_w": 1.0 + nrm(ks[10], (DEPTH, NA_HEAD_DIM), 0.02),
        "na_rpb": nrm(ks[11], (DEPTH, NA_HEADS, 2 * NA_WIN_ROWS - 1, 2 * NA_WIN_COLS - 1), 0.1),
        "conv_w": nrm(ks[12], (DEPTH, CONV_WIDTH, LRU_WIDTH), CONV_WIDTH ** -0.5),
        "conv_b": nrm(ks[13], (DEPTH, LRU_WIDTH), 0.02),
        "lru_w_a": nrm(ks[14], (DEPTH, 2, LRU_BLOCKS, LRU_BLOCK, LRU_BLOCK), LRU_BLOCK ** -0.5),
        "lru_b_a": nrm(ks[15], (DEPTH, 2, LRU_WIDTH), 0.02),
        "lru_w_x": nrm(ks[16], (DEPTH, 2, LRU_BLOCKS, LRU_BLOCK, LRU_BLOCK), LRU_BLOCK ** -0.5),
        "lru_b_x": nrm(ks[17], (DEPTH, 2, LRU_WIDTH), 0.02),
        "lru_lambda": jnp.log(p0) - jnp.log1p(-p0),
        "w_o_attn": nrm(ks[19], (DEPTH, NA_WIDTH, D_MODEL), NA_WIDTH ** -0.5),
        "w_o_lru": nrm(ks[20], (DEPTH, LRU_WIDTH, D_MODEL), LRU_WIDTH ** -0.5),
        "w_out": nrm(ks[21], (DEPTH, D_MODEL, D_MODEL), D_MODEL ** -0.5),
        "peer_w_q": nrm(ks[22], (DEPTH, D_MODEL, PEER_HEADS * PEER_QDIM), D_MODEL ** -0.5),
        "peer_keys": nrm(ks[23], (DEPTH, PEER_HEADS, 2, PEER_NKEYS, PEER_HALF), PEER_HALF ** -0.5),
        "peer_u": nrm(ks[24], (DEPTH, PEER_EXPERTS, D_MODEL), D_MODEL ** -0.5),
        "peer_v": nrm(ks[25], (DEPTH, PEER_EXPERTS, D_MODEL), PEER_HEADS ** -0.5),
    }


def reference(x, c, ctx, c_ctx, w_ada, b_ada, norm1_w, norm2_w, w_in, q_norm_w, k_norm_w, na_rpb,
              conv_w, conv_b, lru_w_a, lru_b_a, lru_w_x, lru_b_x, lru_lambda, w_o_attn, w_o_lru,
              w_out, peer_w_q, peer_keys, peer_u, peer_v):
    B, S, D = x.shape
    C = ctx.shape[1]
    for layer in range(DEPTH):
        mod = jax.nn.silu(c) @ w_ada[layer] + b_ada[layer]
        shift1, scale1, gate1, shift2, scale2, gate2 = jnp.split(mod, 6, axis=-1)
        mod_c = jax.nn.silu(c_ctx) @ w_ada[layer] + b_ada[layer]
        shift1_c, scale1_c = mod_c[:D], mod_c[D:2 * D]

        h = modulate(rms_norm(x, norm1_w[layer]), shift1, scale1)
        z = h @ w_in[layer]
        hc = rms_norm(ctx, norm1_w[layer]) * (1.0 + scale1_c) + shift1_c
        zc = hc @ w_in[layer][:, K0:YR0]

        q = rms_norm(z[..., Q0:K0].reshape(B, S, NA_HEADS, NA_HEAD_DIM), q_norm_w[layer])
        k = rms_norm(z[..., K0:V0].reshape(B, S, NA_HEADS, NA_HEAD_DIM), k_norm_w[layer])
        v = z[..., V0:XR0].reshape(B, S, NA_HEADS, NA_HEAD_DIM)
        k_c = rms_norm(zc[..., :NA_WIDTH].reshape(B, C, NA_HEADS, NA_HEAD_DIM), k_norm_w[layer])
        v_c = zc[..., NA_WIDTH:2 * NA_WIDTH].reshape(B, C, NA_HEADS, NA_HEAD_DIM)
        o_a = neighbourhood_attention(q, k, v, k_c, v_c, na_rpb[layer])

        u = dwconv_centred(z[..., XR0:YR0], conv_w[layer], conv_b[layer])
        u_c = dwconv_centred(zc[..., 2 * NA_WIDTH:], conv_w[layer], conv_b[layer])
        h_r = rglru_bidirectional(u, u_c, lru_w_a[layer], lru_b_a[layer], lru_w_x[layer],
                                  lru_b_x[layer], lru_lambda[layer])
        o_b = h_r.astype(x.dtype) * jax.nn.gelu(z[..., YR0:GA0])

        merged = (jax.nn.sigmoid(z[..., GA0:GB0]) * (o_a @ w_o_attn[layer])
                  + jax.nn.sigmoid(z[..., GB0:]) * (o_b @ w_o_lru[layer]))
        x = x + gate1[:, None, :] * (merged @ w_out[layer])

        h2 = modulate(rms_norm(x, norm2_w[layer]), shift2, scale2)
        y = peer(h2, peer_w_q[layer], peer_keys[layer], peer_u[layer], peer_v[layer])
        x = x + gate2[:, None, :] * y
    return x
```

```python
import functools

import numpy as np
import jax
import jax.numpy as jnp
from jax import lax
from jax.experimental import pallas as pl
from jax.experimental.pallas import tpu as pltpu

F32 = jnp.float32
BF16 = jnp.bfloat16

D_MODEL = 1024
GRID_W = 64
CTX_LEN = 256
NA_HEADS = 8
NA_HEAD_DIM = 64
NA_WIDTH = NA_HEADS * NA_HEAD_DIM
NA_WIN_ROWS = 8
NA_WIN_COLS = 16
LRU_WIDTH = D_MODEL
LRU_BLOCKS = 8
LRU_BLOCK = LRU_WIDTH // LRU_BLOCKS
CONV_WIDTH = 4
CONV_LEFT = 2
LRU_C = 8.0
PEER_HEADS = 8
PEER_NKEYS = 128
PEER_EXPERTS = PEER_NKEYS * PEER_NKEYS
PEER_QDIM = 256
PEER_HALF = PEER_QDIM // 2
PEER_TOPK = 16
EPS = 1e-6
NEG_INF = -1e30

Q0 = 0
K0 = Q0 + NA_WIDTH
V0 = K0 + NA_WIDTH
XR0 = V0 + NA_WIDTH
YR0 = XR0 + LRU_WIDTH
GA0 = YR0 + LRU_WIDTH
GB0 = GA0 + D_MODEL
IN_COLS = GB0 + D_MODEL

V7X_LANES = 128
V7X_SUBLANES = 8
V7X_VMEM_BYTES = 64 * 1024 * 1024

MOD_ROWS = 8
ATT_QROWS = 8
ATT_KROWS = 16
LRU_CHUNK = 256
HALO = V7X_SUBLANES

_NT = (((1,), (1,)), ((), ()))


def _cparams(sem, vmem_mb):
    return pltpu.CompilerParams(dimension_semantics=sem, vmem_limit_bytes=vmem_mb * 1024 * 1024)


def _resident(shape, index_map):
    return pl.BlockSpec(shape, index_map, pipeline_mode=pl.Buffered(1))


def _gelu(x):
    return jax.nn.gelu(x)


def _sigmoid(x):
    return jax.nn.sigmoid(x)


def _mod_kernel(c_ref, w_ref, b_ref, o_ref):
    cs = c_ref[...]
    s = cs * _sigmoid(cs)
    o_ref[...] = jnp.dot(s, w_ref[...], preferred_element_type=F32,
                         precision=lax.Precision.HIGHEST) + b_ref[...]


def _mod(c_rows, w_ada, b_ada):
    n = w_ada.shape[1]
    tn = 1536
    return pl.pallas_call(
        _mod_kernel,
        grid=(n // tn,),
        in_specs=[pl.BlockSpec((MOD_ROWS, D_MODEL), lambda j: (0, 0)),
                  pl.BlockSpec((D_MODEL, tn), lambda j: (0, j)),
                  pl.BlockSpec((1, tn), lambda j: (0, j))],
        out_specs=pl.BlockSpec((MOD_ROWS, tn), lambda j: (0, j)),
        out_shape=jax.ShapeDtypeStruct((MOD_ROWS, n), F32),
        compiler_params=_cparams(("arbitrary",), 40),
        name="mod",
    )(c_rows, w_ada, b_ada.reshape(1, n))


def _normed_input(x_ref, mod_ref, n1_ref):
    x = x_ref[...]
    ms = jnp.mean(x * x, axis=-1, keepdims=True)
    h = x * lax.rsqrt(ms + EPS) * n1_ref[...]
    shift = mod_ref[:, 0:D_MODEL]
    scale = mod_ref[:, D_MODEL:2 * D_MODEL]
    return (h * (1.0 + scale) + shift).astype(BF16)


def _head_norm(z, g_ref, w_row):
    ms = jnp.dot(z * z, g_ref[...], preferred_element_type=F32, precision=lax.Precision.HIGHEST)
    return z * lax.rsqrt(ms + EPS) * w_row


def _inproj_kernel(x_ref, mod_ref, n1_ref, w_ref, g_ref, qn_ref, kn_ref,
                   qkv_ref, xr_ref, gy_ref, sga_ref, sgb_ref):
    hb = _normed_input(x_ref, mod_ref, n1_ref)
    seg = lambda lo, hi: jnp.dot(hb, w_ref[:, lo:hi], preferred_element_type=F32)
    qkv_ref[:, Q0:K0] = _head_norm(seg(Q0, K0), g_ref, qn_ref[...]).astype(BF16)
    qkv_ref[:, K0:V0] = _head_norm(seg(K0, V0), g_ref, kn_ref[...]).astype(BF16)
    qkv_ref[:, V0:XR0] = seg(V0, XR0).astype(BF16)
    xr_ref[...] = seg(XR0, YR0)
    gy_ref[...] = _gelu(seg(YR0, GA0))
    sga_ref[...] = _sigmoid(seg(GA0, GB0))
    sgb_ref[...] = _sigmoid(seg(GB0, IN_COLS))


def _inproj_ctx_kernel(x_ref, mod_ref, n1_ref, w_ref, g_ref, kn_ref, kv_ref, xr_ref):
    hb = _normed_input(x_ref, mod_ref, n1_ref)
    seg = lambda lo, hi: jnp.dot(hb, w_ref[:, lo:hi], preferred_element_type=F32)
    kv_ref[:, 0:NA_WIDTH] = _head_norm(seg(K0, V0), g_ref, kn_ref[...]).astype(BF16)
    kv_ref[:, NA_WIDTH:2 * NA_WIDTH] = seg(V0, XR0).astype(BF16)
    xr_ref[...] = seg(XR0, YR0)


def _head_mean_matrix():
    head = np.arange(NA_WIDTH) // NA_HEAD_DIM
    return jnp.asarray((head[:, None] == head[None, :]).astype(np.float32) / NA_HEAD_DIM)


def _inproj(x2, mod3, n1, w_in_b, qn_row, kn_row, rows_per_batch):
    t = x2.shape[0]
    tm = 512
    per = rows_per_batch // tm
    row = lambda i: (i, 0)
    const = lambda i: (0, 0)
    wide = pl.BlockSpec((tm, D_MODEL), row)
    return pl.pallas_call(
        _inproj_kernel,
        grid=(t // tm,),
        in_specs=[wide,
                  pl.BlockSpec((None, 1, 6 * D_MODEL), lambda i: (i // per, 0, 0)),
                  _resident((1, D_MODEL), const),
                  _resident((D_MODEL, IN_COLS), const),
                  _resident((NA_WIDTH, NA_WIDTH), const),
                  _resident((1, NA_WIDTH), const),
                  _resident((1, NA_WIDTH), const)],
        out_specs=[pl.BlockSpec((tm, 3 * NA_WIDTH), row), wide, wide, wide, wide],
        out_shape=[jax.ShapeDtypeStruct((t, 3 * NA_WIDTH), BF16)] + [jax.ShapeDtypeStruct((t, D_MODEL), F32)] * 4,
        compiler_params=_cparams(("parallel",), 56),
        name="inproj",
    )(x2, mod3, n1, w_in_b, _head_mean_matrix(), qn_row, kn_row)


def _inproj_ctx(ctx2, mod3, n1, w_in_b, kn_row, ctx_mod_row):
    t = ctx2.shape[0]
    tm = 512
    row = lambda i: (i, 0)
    const = lambda i: (0, 0)
    wide = pl.BlockSpec((tm, D_MODEL), row)
    return pl.pallas_call(
        _inproj_ctx_kernel,
        grid=(t // tm,),
        in_specs=[wide,
                  pl.BlockSpec((None, 1, 6 * D_MODEL), lambda i: (ctx_mod_row, 0, 0)),
                  _resident((1, D_MODEL), const),
                  _resident((D_MODEL, IN_COLS), const),
                  _resident((NA_WIDTH, NA_WIDTH), const),
                  _resident((1, NA_WIDTH), const)],
        out_specs=[wide, wide],
        out_shape=[jax.ShapeDtypeStruct((t, 2 * NA_WIDTH), BF16), jax.ShapeDtypeStruct((t, D_MODEL), F32)],
        compiler_params=_cparams(("parallel",), 56),
        name="inproj_ctx",
    )(ctx2, mod3, n1, w_in_b, _head_mean_matrix(), kn_row)


def _attn_bias_tables(rpb, rows):
    wr, wc, w = NA_WIN_ROWS, NA_WIN_COLS, GRID_W
    col = np.arange(w)
    col0 = np.clip(col - wc // 2, 0, w - wc)
    col_mask = (col[None, :] >= col0[:, None]) & (col[None, :] < col0[:, None] + wc)
    dc = np.clip(col[None, :] - col[:, None], -(wc - 1), wc - 1) + wc - 1
    tab = jnp.where(col_mask[None, None], rpb[:, :, dc], NEG_INF)
    n_blocks = rows // ATT_QROWS
    out = []
    for rb in (0, 1, n_blocks - 1):
        r = rb * ATT_QROWS + np.arange(ATT_QROWS)
        ks = int(np.clip(rb * ATT_QROWS - wr // 2, 0, rows - ATT_KROWS))
        kr = ks + np.arange(ATT_KROWS)
        row0 = np.clip(r - wr // 2, 0, rows - wr)
        valid = (kr[None, :] >= row0[:, None]) & (kr[None, :] < row0[:, None] + wr)
        dr = np.clip(kr[None, :] - r[:, None], -(wr - 1), wr - 1) + wr - 1
        blk = jnp.where(valid[None, :, :, None, None], tab[:, dr], NEG_INF)
        out.append(blk.transpose(0, 1, 3, 2, 4).reshape(NA_HEADS, ATT_QROWS * w, ATT_KROWS * w))
    return jnp.stack(out)


def _attn_kernel(q_ref, k_ref, v_ref, kc_ref, vc_ref, bias_ref, o_ref, *, rows):
    rb = pl.program_id(2)
    ks = jnp.clip(rb * ATT_QROWS - NA_WIN_ROWS // 2, 0, rows - ATT_KROWS) * GRID_W
    ks = pl.multiple_of(ks, GRID_W)
    nk = ATT_KROWS * GRID_W
    q = q_ref[...]
    kw = k_ref[pl.ds(ks, nk), :]
    vw = v_ref[pl.ds(ks, nk), :]
    kc = kc_ref[...]
    vc = vc_ref[...]
    lane = lax.broadcasted_iota(jnp.int32, (1, V7X_LANES), 1)
    acc = jnp.zeros(q.shape, F32)
    for hh in range(V7X_LANES // NA_HEAD_DIM):
        sel = (lane // NA_HEAD_DIM == hh).astype(BF16)
        qh = q * sel
        s = lax.dot_general(qh, kw, _NT, preferred_element_type=F32) + bias_ref[hh]
        sc = lax.dot_general(qh, kc, _NT, preferred_element_type=F32)
        mx = jnp.maximum(jnp.max(s, axis=-1, keepdims=True), jnp.max(sc, axis=-1, keepdims=True))
        p = jnp.exp(s - mx)
        pc = jnp.exp(sc - mx)
        den = jnp.sum(p, axis=-1, keepdims=True) + jnp.sum(pc, axis=-1, keepdims=True)
        o = (jnp.dot(p.astype(BF16), vw * sel, preferred_element_type=F32)
             + jnp.dot(pc.astype(BF16), vc * sel, preferred_element_type=F32))
        acc = acc + o / den
    o_ref[...] = acc.astype(BF16)


def _attention(qkv, kv_c, bias, batch, seq):
    rows = seq // GRID_W
    n_rb = rows // ATT_QROWS
    tq = ATT_QROWS * GRID_W
    pairs = NA_WIDTH // V7X_LANES
    kcol, vcol = K0 // V7X_LANES, V0 // V7X_LANES
    variant = lambda rb: jnp.where(rb == 0, 0, jnp.where(rb == n_rb - 1, 2, 1))
    return pl.pallas_call(
        functools.partial(_attn_kernel, rows=rows),
        grid=(pairs, batch, n_rb),
        in_specs=[pl.BlockSpec((tq, V7X_LANES), lambda hp, b, rb: (b * n_rb + rb, hp)),
                  pl.BlockSpec((seq, V7X_LANES), lambda hp, b, rb: (b, kcol + hp)),
                  pl.BlockSpec((seq, V7X_LANES), lambda hp, b, rb: (b, vcol + hp)),
                  pl.BlockSpec((CTX_LEN, V7X_LANES), lambda hp, b, rb: (b, hp)),
                  pl.BlockSpec((CTX_LEN, V7X_LANES), lambda hp, b, rb: (b, pairs + hp)),
                  pl.BlockSpec((None, 2, tq, ATT_KROWS * GRID_W), lambda hp, b, rb: (variant(rb), hp, 0, 0))],
        out_specs=pl.BlockSpec((tq, V7X_LANES), lambda hp, b, rb: (b * n_rb + rb, hp)),
        out_shape=jax.ShapeDtypeStruct((batch * seq, NA_WIDTH), BF16),
        compiler_params=_cparams(("parallel", "parallel", "arbitrary"), 48),
        name="attn",
    )(qkv, qkv, qkv, kv_c, kv_c, bias)


def _lru_kernel(*refs, reverse, n_chunks):
    if reverse:
        (x_ref, xc_ref, prev_ref, next_ref, cw_ref, cb_ref, wg_ref, bg_ref, lam_ref,
         h_ref, win_ref, a_ref, b_ref, carry_ref) = refs
    else:
        (x_ref, xc_ref, prev_ref, next_ref, cw_ref, cb_ref, wg_ref, bg_ref, lam_ref, hb_ref, gy_ref,
         h_ref, win_ref, a_ref, b_ref, carry_ref) = refs
    c = pl.program_id(1)
    is_ctx = c == 0
    chunk = (n_chunks - c) if reverse else (c - 1)
    n = LRU_CHUNK

    has_prev = jnp.logical_and(c > 0, chunk > 0)
    has_next = jnp.logical_and(c > 0, chunk < n_chunks - 1)
    win_ref[0:HALO, :] = jnp.where(has_prev, prev_ref[...], 0.0)
    win_ref[HALO:HALO + n, :] = jnp.where(is_ctx, xc_ref[...], x_ref[...])
    win_ref[HALO + n:2 * HALO + n, :] = jnp.where(has_next, next_ref[...], 0.0)
    u = cb_ref[...]
    for j in range(CONV_WIDTH):
        off = HALO - CONV_LEFT + j
        u = u + cw_ref[j:j + 1, :] * win_ref[off:off + n, :]

    ub = u.astype(BF16)
    lam = lam_ref[...]
    softplus = jnp.maximum(-lam, 0.0) + jnp.log(1.0 + jnp.exp(-jnp.abs(lam)))
    for blk in range(LRU_BLOCKS):
        lo, hi = blk * LRU_BLOCK, (blk + 1) * LRU_BLOCK
        g = jnp.dot(ub[:, lo:hi], wg_ref[blk], preferred_element_type=F32)
        r = _sigmoid(g[:, 0:LRU_BLOCK] + bg_ref[0:1, lo:hi])
        i = _sigmoid(g[:, LRU_BLOCK:2 * LRU_BLOCK] + bg_ref[1:2, lo:hi])
        log_a = -LRU_C * r * softplus[:, lo:hi]
        a = jnp.exp(log_a)
        beta = jnp.sqrt(1.0 - a * a)
        a_ref[:, lo:hi] = a
        b_ref[:, lo:hi] = beta * (i * u[:, lo:hi])

    a = a_ref[...]
    b = b_ref[...]
    pos = lax.broadcasted_iota(jnp.int32, (n, 1), 0) % V7X_SUBLANES
    for s in (1, 2, 4):
        if reverse:
            a_n = pltpu.roll(a, n - s, axis=0)
            b_n = pltpu.roll(b, n - s, axis=0)
            ok = pos < V7X_SUBLANES - s
        else:
            a_n = pltpu.roll(a, s, axis=0)
            b_n = pltpu.roll(b, s, axis=0)
            ok = pos >= s
        b = jnp.where(ok, a * b_n + b, b)
        a = jnp.where(ok, a * a_n, a)
    a_ref[...] = a
    b_ref[...] = b

    @pl.when(is_ctx)
    def _():
        carry_ref[...] = jnp.zeros_like(carry_ref)

    groups = n // V7X_SUBLANES
    edge = 0 if reverse else V7X_SUBLANES - 1

    def group_step(gi, carry):
        g = (groups - 1 - gi) if reverse else gi
        row = pl.multiple_of(g * V7X_SUBLANES, V7X_SUBLANES)
        hg = a_ref[pl.ds(row, V7X_SUBLANES), :] * carry + b_ref[pl.ds(row, V7X_SUBLANES), :]
        if reverse:
            h_ref[pl.ds(row, V7X_SUBLANES), :] = hg
        else:
            h_ref[pl.ds(row, V7X_SUBLANES), :] = ((hg + hb_ref[pl.ds(row, V7X_SUBLANES), :])
                                                  * gy_ref[pl.ds(row, V7X_SUBLANES), :])
        return hg[edge:edge + 1, :]

    carry_ref[...] = lax.fori_loop(0, groups, group_step, carry_ref[...])


def _lru_pass(xr, xr_c, conv_w, conv_b, w_gate, b_gate, lam, batch, seq, reverse, h_bwd=None, gy=None):
    n = LRU_CHUNK
    n_chunks = seq // n
    per_halo = n // HALO

    def chunk_of(c):
        return jnp.where(c == 0, (n_chunks - 1) if reverse else 0, (n_chunks - c) if reverse else (c - 1))

    cur = lambda b, c: (b * n_chunks + chunk_of(c), 0)
    prev = lambda b, c: (jnp.maximum((b * n_chunks + chunk_of(c)) * per_halo - 1, 0), 0)
    nxt = lambda b, c: (jnp.minimum((b * n_chunks + chunk_of(c) + 1) * per_halo, batch * seq // HALO - 1), 0)
    const2 = lambda b, c: (0, 0)
    const3 = lambda b, c: (0, 0, 0)
    big = pl.BlockSpec((n, LRU_WIDTH), cur)
    in_specs = [big,
                pl.BlockSpec((n, LRU_WIDTH), lambda b, c: (b, 0)),
                pl.BlockSpec((HALO, LRU_WIDTH), prev),
                pl.BlockSpec((HALO, LRU_WIDTH), nxt),
                _resident((CONV_WIDTH, LRU_WIDTH), const2),
                _resident((1, LRU_WIDTH), const2),
                _resident((LRU_BLOCKS, LRU_BLOCK, 2 * LRU_BLOCK), const3),
                _resident((2, LRU_WIDTH), const2),
                _resident((1, LRU_WIDTH), const2)]
    args = [xr, xr_c, xr, xr, conv_w, conv_b, w_gate, b_gate, lam]
    if not reverse:
        in_specs += [big, big]
        args += [h_bwd, gy]
    return pl.pallas_call(
        functools.partial(_lru_kernel, reverse=reverse, n_chunks=n_chunks),
        grid=(batch, n_chunks + 1),
        in_specs=in_specs,
        out_specs=big,
        out_shape=jax.ShapeDtypeStruct((batch * seq, LRU_WIDTH), F32),
        scratch_shapes=[pltpu.VMEM((n + 2 * HALO, LRU_WIDTH), F32),
                        pltpu.VMEM((n, LRU_WIDTH), F32),
                        pltpu.VMEM((n, LRU_WIDTH), F32),
                        pltpu.VMEM((1, LRU_WIDTH), F32)],
        compiler_params=_cparams(("parallel", "arbitrary"), 40),
        name="lru_bwd" if reverse else "lru_fwd",
    )(*args)


def _merge_kernel(oa_ref, ob_ref, sga_ref, sgb_ref, x_ref, mod_ref, woa_ref, wol_ref, wout_ref,
                  n2_ref, wq_ref, keys_ref, x1_ref, h2_ref, st_ref):
    ma = jnp.dot(oa_ref[...], woa_ref[...], preferred_element_type=F32)
    mb = jnp.dot(ob_ref[...].astype(BF16), wol_ref[...], preferred_element_type=F32)
    merged = sga_ref[...] * ma + sgb_ref[...] * mb
    gate1 = mod_ref[:, 2 * D_MODEL:3 * D_MODEL]
    x1 = x_ref[...] + gate1 * jnp.dot(merged.astype(BF16), wout_ref[...], preferred_element_type=F32)
    x1_ref[...] = x1
    ms = jnp.mean(x1 * x1, axis=-1, keepdims=True)
    h = x1 * lax.rsqrt(ms + EPS) * n2_ref[...]
    shift2 = mod_ref[:, 3 * D_MODEL:4 * D_MODEL]
    scale2 = mod_ref[:, 4 * D_MODEL:5 * D_MODEL]
    h2 = (h * (1.0 + scale2) + shift2).astype(BF16)
    h2_ref[...] = h2
    qp = jnp.dot(h2, wq_ref[...], preferred_element_type=F32).astype(BF16)
    for hp in range(2 * PEER_HEADS):
        st_ref[hp] = lax.dot_general(keys_ref[hp], qp[:, hp * PEER_HALF:(hp + 1) * PEER_HALF], _NT,
                                     preferred_element_type=F32)


def _merge(o_a, o_b, sga, sgb, x2, mod3, w_oa, w_ol, w_out, n2, w_q, keys, rows_per_batch):
    t = x2.shape[0]
    tm = 512
    per = rows_per_batch // tm
    row = lambda i: (i, 0)
    const = lambda i: (0, 0)
    wide = pl.BlockSpec((tm, D_MODEL), row)
    nq = PEER_HEADS * PEER_QDIM
    return pl.pallas_call(
        _merge_kernel,
        grid=(t // tm,),
        in_specs=[pl.BlockSpec((tm, NA_WIDTH), row), wide, wide, wide, wide,
                  pl.BlockSpec((None, 1, 6 * D_MODEL), lambda i: (i // per, 0, 0)),
                  _resident((NA_WIDTH, D_MODEL), const),
                  _resident((LRU_WIDTH, D_MODEL), const),
                  _resident((D_MODEL, D_MODEL), const),
                  _resident((1, D_MODEL), const),
                  _resident((D_MODEL, nq), const),
                  _resident((2 * PEER_HEADS, PEER_NKEYS, PEER_HALF), lambda i: (0, 0, 0))],
        out_specs=[wide, wide, pl.BlockSpec((2 * PEER_HEADS, PEER_NKEYS, tm), lambda i: (0, 0, i))],
        out_shape=[jax.ShapeDtypeStruct((t, D_MODEL), F32),
                   jax.ShapeDtypeStruct((t, D_MODEL), BF16),
                   jax.ShapeDtypeStruct((2 * PEER_HEADS, PEER_NKEYS, t), F32)],
        compiler_params=_cparams(("parallel",), 56),
        name="merge",
    )(o_a, o_b, sga, sgb, x2, mod3, w_oa, w_ol, w_out, n2, w_q, keys)


def _top_values(x, k):
    vals = []
    for _ in range(k):
        m = jnp.max(x, axis=0, keepdims=True)
        vals.append(m)
        x = jnp.where(x >= m, NEG_INF, x)
    return vals


def _select_kernel(st_ref, a1_ref, p1_ref, s2_ref, p2_ref, tau_ref):
    k = PEER_TOPK
    for h in range(PEER_HEADS):
        s1 = st_ref[2 * h]
        s2 = st_ref[2 * h + 1]
        v1 = _top_values(s1, k)
        v2 = _top_values(s2, k)
        slot = lax.broadcasted_iota(jnp.int32, (k, s2.shape[1]), 0)
        v2_all = jnp.zeros((k, s2.shape[1]), F32)
        for b in range(k):
            v2_all = jnp.where(slot == b, v2[b], v2_all)
        cand0 = jnp.concatenate([v1[a] + v2_all for a in range(k)], axis=0)
        cand = cand0
        tau = jnp.full_like(v1[0], NEG_INF)
        seen = jnp.zeros_like(v1[0])
        for _ in range(k):
            m = jnp.max(cand, axis=0, keepdims=True)
            hit = cand >= m
            tau = jnp.where(seen < k, m, tau)
            seen = seen + jnp.sum(hit.astype(F32), axis=0, keepdims=True)
            cand = jnp.where(hit, NEG_INF, cand)
        top = v1[0] + v2[0]
        z = jnp.sum(jnp.where(cand0 >= tau, jnp.exp(cand0 - top), 0.0), axis=0, keepdims=True)
        a1_ref[h] = jnp.where(s1 >= v1[k - 1], s1, NEG_INF)
        p1_ref[h] = jnp.exp(s1 - v1[0])
        s2_ref[h] = s2
        p2_ref[h] = jnp.exp(s2 - v2[0]) / z
        tau_ref[h] = tau


def _select(st):
    t = st.shape[2]
    tt = 256
    blk = pl.BlockSpec((PEER_HEADS, PEER_NKEYS, tt), lambda i: (0, 0, i))
    full = jax.ShapeDtypeStruct((PEER_HEADS, PEER_NKEYS, t), F32)
    return pl.pallas_call(
        _select_kernel,
        grid=(t // tt,),
        in_specs=[pl.BlockSpec((2 * PEER_HEADS, PEER_NKEYS, tt), lambda i: (0, 0, i))],
        out_specs=[blk, blk, blk, blk, pl.BlockSpec((PEER_HEADS, 1, tt), lambda i: (0, 0, i))],
        out_shape=[full, full, full, full, jax.ShapeDtypeStruct((PEER_HEADS, 1, t), F32)],
        compiler_params=_cparams(("parallel",), 40),
        name="select",
    )(st)


EXP_TOK = 512
EXP_BLK = 1024
EXP_I = EXP_BLK // PEER_NKEYS


def _experts_kernel(h2_ref, u_ref, vt_ref, a1_ref, p1_ref, s2_ref, p2_ref, tau_ref, x1_ref, mod_ref,
                    o_ref, at_ref, m_ref, yt_ref):
    eb = pl.program_id(1)

    @pl.when(eb == 0)
    def _():
        yt_ref[...] = jnp.zeros_like(yt_ref)

    at_ref[...] = lax.dot_general(u_ref[...], h2_ref[...], _NT, preferred_element_type=F32)
    i0 = pl.multiple_of(eb * EXP_I, EXP_I)
    for lt in range(EXP_TOK // V7X_LANES):
        lanes = slice(lt * V7X_LANES, (lt + 1) * V7X_LANES)
        a1_rows = [a1_ref[h, pl.ds(i0, EXP_I), lanes] for h in range(PEER_HEADS)]
        p1_rows = [p1_ref[h, pl.ds(i0, EXP_I), lanes] for h in range(PEER_HEADS)]
        for il in range(EXP_I):
            w = jnp.zeros((PEER_NKEYS, V7X_LANES), F32)
            for h in range(PEER_HEADS):
                a1 = a1_rows[h][il:il + 1, :]
                p1 = p1_rows[h][il:il + 1, :]
                picked = (a1 + s2_ref[h, :, lanes]) >= tau_ref[h, :, lanes]
                w = w + jnp.where(picked, p1 * p2_ref[h, :, lanes], 0.0)
            rows = slice(il * PEER_NKEYS, (il + 1) * PEER_NKEYS)
            m_ref[rows, lanes] = (w * _gelu(at_ref[rows, lanes])).astype(BF16)
    yt_ref[...] += jnp.dot(vt_ref[...], m_ref[...], preferred_element_type=F32)

    @pl.when(eb == pl.num_programs(1) - 1)
    def _():
        gate2 = mod_ref[:, 5 * D_MODEL:6 * D_MODEL]
        o_ref[...] = x1_ref[...] + gate2 * yt_ref[...].T


def _experts(h2, u_b, vt_b, a1, p1, s2, p2, tau, x1, mod3, rows_per_batch):
    t = h2.shape[0]
    per = rows_per_batch // EXP_TOK
    tok = lambda i, e: (i, 0)
    sel = pl.BlockSpec((PEER_HEADS, PEER_NKEYS, EXP_TOK), lambda i, e: (0, 0, i))
    return pl.pallas_call(
        _experts_kernel,
        grid=(t // EXP_TOK, PEER_EXPERTS // EXP_BLK),
        in_specs=[pl.BlockSpec((EXP_TOK, D_MODEL), tok),
                  pl.BlockSpec((EXP_BLK, D_MODEL), lambda i, e: (e, 0)),
                  pl.BlockSpec((D_MODEL, EXP_BLK), lambda i, e: (0, e)),
                  sel, sel, sel, sel,
                  pl.BlockSpec((PEER_HEADS, 1, EXP_TOK), lambda i, e: (0, 0, i)),
                  pl.BlockSpec((EXP_TOK, D_MODEL), tok),
                  pl.BlockSpec((None, 1, 6 * D_MODEL), lambda i, e: (i // per, 0, 0))],
        out_specs=pl.BlockSpec((EXP_TOK, D_MODEL), tok),
        out_shape=jax.ShapeDtypeStruct((t, D_MODEL), F32),
        scratch_shapes=[pltpu.VMEM((EXP_BLK, EXP_TOK), F32),
                        pltpu.VMEM((EXP_BLK, EXP_TOK), BF16),
                        pltpu.VMEM((D_MODEL, EXP_TOK), F32)],
        compiler_params=_cparams(("parallel", "arbitrary"), 56),
        name="experts",
    )(h2, u_b, vt_b, a1, p1, s2, p2, tau, x1, mod3)


def _layer(x, c, ctx, c_ctx, w_ada, b_ada, norm1_w, norm2_w, w_in, q_norm_w, k_norm_w, na_rpb,
           conv_w, conv_b, lru_w_a, lru_b_a, lru_w_x, lru_b_x, lru_lambda, w_o_attn, w_o_lru,
           w_out, peer_w_q, peer_keys, peer_u, peer_v):
    batch, seq, d = x.shape
    n_ctx = ctx.shape[1]
    assert d == D_MODEL and n_ctx == CTX_LEN and seq % (ATT_QROWS * GRID_W) == 0 and batch < MOD_ROWS
    x2 = x.reshape(batch * seq, d)
    ctx2 = ctx.reshape(batch * n_ctx, d)

    c_rows = jnp.zeros((MOD_ROWS, d), F32).at[:batch].set(c).at[batch].set(c_ctx)
    mod3 = _mod(c_rows, w_ada, b_ada).reshape(MOD_ROWS, 1, 6 * d)

    w_in_b = w_in.astype(BF16)
    n1 = norm1_w.reshape(1, d)
    qn_row = (jnp.tile(q_norm_w, NA_HEADS) * (NA_HEAD_DIM ** -0.5)).reshape(1, NA_WIDTH)
    kn_row = jnp.tile(k_norm_w, NA_HEADS).reshape(1, NA_WIDTH)
    qkv, xr, gy, sga, sgb = _inproj(x2, mod3, n1, w_in_b, qn_row, kn_row, seq)
    kv_c, xr_c = _inproj_ctx(ctx2, mod3, n1, w_in_b, kn_row, batch)

    bias = _attn_bias_tables(na_rpb, seq // GRID_W)
    o_a = _attention(qkv, kv_c, bias, batch, seq)

    cb = conv_b.reshape(1, LRU_WIDTH)
    outs = []
    h_bwd = None
    for direction in (1, 0):
        w_gate = jnp.concatenate([lru_w_a[direction], lru_w_x[direction]], axis=-1).astype(BF16)
        b_gate = jnp.stack([lru_b_a[direction], lru_b_x[direction]])
        lam = lru_lambda[direction].reshape(1, LRU_WIDTH)
        if direction == 1:
            h_bwd = _lru_pass(xr, xr_c, conv_w, cb, w_gate, b_gate, lam, batch, seq, True)
        else:
            o_b = _lru_pass(xr, xr_c, conv_w, cb, w_gate, b_gate, lam, batch, seq, False, h_bwd, gy)

    keys = peer_keys.reshape(2 * PEER_HEADS, PEER_NKEYS, PEER_HALF).astype(BF16)
    x1, h2, st = _merge(o_a, o_b, sga, sgb, x2, mod3, w_o_attn.astype(BF16), w_o_lru.astype(BF16),
                        w_out.astype(BF16), norm2_w.reshape(1, d), peer_w_q.astype(BF16), keys, seq)
    a1, p1, s2, p2, tau = _select(st)
    out = _experts(h2, peer_u.astype(BF16), peer_v.astype(BF16).T, a1, p1, s2, p2, tau, x1, mod3, seq)
    return out.reshape(batch, seq, d)


def kernel(x, c, ctx, c_ctx, w_ada, b_ada, norm1_w, norm2_w, w_in, q_norm_w, k_norm_w, na_rpb, conv_w, conv_b, lru_w_a, lru_b_a, lru_w_x, lru_b_x, lru_lambda, w_o_attn, w_o_lru, w_out, peer_w_q, peer_keys, peer_u, peer_v):
    depth = w_ada.shape[0]
    for layer in range(depth):
        x = _layer(x, c, ctx, c_ctx, w_ada[layer], b_ada[layer], norm1_w[layer], norm2_w[layer],
                   w_in[layer], q_norm_w[layer], k_norm_w[layer], na_rpb[layer], conv_w[layer],
                   conv_b[layer], lru_w_a[layer], lru_b_a[layer], lru_w_x[layer], lru_b_x[layer],
                   lru_lambda[layer], w_o_attn[layer], w_o_lru[layer], w_out[layer], peer_w_q[layer],
                   peer_keys[layer], peer_u[layer], peer_v[layer])
    return x
```

```python
import functools

import numpy as np
import jax
import jax.numpy as jnp
from jax import lax
from jax.experimental import pallas as pl
from jax.experimental.pallas import tpu as pltpu

F32 = jnp.float32
BF16 = jnp.bfloat16

D_MODEL = 1024
GRID_W = 64
CTX_LEN = 256
NA_HEADS = 8
NA_HEAD_DIM = 64
NA_WIDTH = NA_HEADS * NA_HEAD_DIM
NA_WIN_ROWS = 8
NA_WIN_COLS = 16
LRU_WIDTH = D_MODEL
LRU_BLOCKS = 8
LRU_BLOCK = LRU_WIDTH // LRU_BLOCKS
CONV_WIDTH = 4
CONV_LEFT = 2
LRU_C = 8.0
PEER_HEADS = 8
PEER_NKEYS = 128
PEER_EXPERTS = PEER_NKEYS * PEER_NKEYS
PEER_QDIM = 256
PEER_HALF = PEER_QDIM // 2
PEER_TOPK = 16
EPS = 1e-6
NEG_INF = -1e30

Q0 = 0
K0 = Q0 + NA_WIDTH
V0 = K0 + NA_WIDTH
XR0 = V0 + NA_WIDTH
YR0 = XR0 + LRU_WIDTH
GA0 = YR0 + LRU_WIDTH
GB0 = GA0 + D_MODEL
IN_COLS = GB0 + D_MODEL

V7X_LANES = 128
V7X_SUBLANES = 8
BF16_SUBLANES = 2 * V7X_SUBLANES
V7X_VMEM_BYTES = 64 * 1024 * 1024

MOD_ROWS = 8
ATT_QROWS = 8
ATT_KROWS = 16
LRU_CHUNK = 256
HALO = V7X_SUBLANES

_NT = (((1,), (1,)), ((), ()))


def _cparams(sem, vmem_mb, flags=None):
    return pltpu.CompilerParams(dimension_semantics=sem, vmem_limit_bytes=vmem_mb * 1024 * 1024, flags=flags)


def _resident(shape, index_map):
    return pl.BlockSpec(shape, index_map, pipeline_mode=pl.Buffered(1))


def _gelu(x):
    return jax.nn.gelu(x)


def _sigmoid(x):
    return jax.nn.sigmoid(x)


_GELU_A = -2.0 * float(np.sqrt(2.0 / np.pi) * np.log2(np.e))
_GELU_B = _GELU_A * 0.044715


def _gelu_bf16(x):
    xb = x.astype(BF16)
    z = (xb * xb * _GELU_B + _GELU_A) * xb
    return xb / (1.0 + jnp.exp2(z))


def _mod_kernel(c_ref, w_ref, b_ref, o_ref):
    cs = c_ref[...]
    s = cs * _sigmoid(cs)
    o_ref[...] = jnp.dot(s, w_ref[...], preferred_element_type=F32,
                         precision=lax.Precision.HIGHEST) + b_ref[...]


def _mod(c_rows, w_ada, b_ada):
    n = w_ada.shape[1]
    tn = 1536
    return pl.pallas_call(
        _mod_kernel,
        grid=(n // tn,),
        in_specs=[pl.BlockSpec((MOD_ROWS, D_MODEL), lambda j: (0, 0)),
                  pl.BlockSpec((D_MODEL, tn), lambda j: (0, j)),
                  pl.BlockSpec((1, tn), lambda j: (0, j))],
        out_specs=pl.BlockSpec((MOD_ROWS, tn), lambda j: (0, j)),
        out_shape=jax.ShapeDtypeStruct((MOD_ROWS, n), F32),
        compiler_params=_cparams(("arbitrary",), 40),
        name="mod",
    )(c_rows, w_ada, b_ada.reshape(1, n))


def _normed_input(x_ref, mod_ref, n1_ref):
    x = x_ref[...]
    ms = jnp.mean(x * x, axis=-1, keepdims=True)
    h = x * lax.rsqrt(ms + EPS) * n1_ref[...]
    shift = mod_ref[:, 0:D_MODEL]
    scale = mod_ref[:, D_MODEL:2 * D_MODEL]
    return (h * (1.0 + scale) + shift).astype(BF16)


def _head_norm(z, g_ref, w_row):
    ms = jnp.dot(z * z, g_ref[...], preferred_element_type=F32, precision=lax.Precision.HIGHEST)
    return z * lax.rsqrt(ms + EPS) * w_row


def _inproj_kernel(x_ref, mod_ref, n1_ref, w_ref, g_ref, qn_ref, kn_ref,
                   qkv_ref, xr_ref, gy_ref, sga_ref, sgb_ref):
    hb = _normed_input(x_ref, mod_ref, n1_ref)
    seg = lambda lo, hi: jnp.dot(hb, w_ref[:, lo:hi], preferred_element_type=F32)
    qkv_ref[:, Q0:K0] = _head_norm(seg(Q0, K0), g_ref, qn_ref[...]).astype(BF16)
    qkv_ref[:, K0:V0] = _head_norm(seg(K0, V0), g_ref, kn_ref[...]).astype(BF16)
    qkv_ref[:, V0:XR0] = seg(V0, XR0).astype(BF16)
    xr_ref[...] = seg(XR0, YR0)
    gy_ref[...] = _gelu(seg(YR0, GA0))
    sga_ref[...] = _sigmoid(seg(GA0, GB0))
    sgb_ref[...] = _sigmoid(seg(GB0, IN_COLS))


def _inproj_ctx_kernel(x_ref, mod_ref, n1_ref, w_ref, g_ref, kn_ref, kv_ref, xr_ref):
    hb = _normed_input(x_ref, mod_ref, n1_ref)
    seg = lambda lo, hi: jnp.dot(hb, w_ref[:, lo:hi], preferred_element_type=F32)
    kv_ref[:, 0:NA_WIDTH] = _head_norm(seg(K0, V0), g_ref, kn_ref[...]).astype(BF16)
    kv_ref[:, NA_WIDTH:2 * NA_WIDTH] = seg(V0, XR0).astype(BF16)
    xr_ref[...] = seg(XR0, YR0)


def _head_mean_matrix():
    head = np.arange(NA_WIDTH) // NA_HEAD_DIM
    return jnp.asarray((head[:, None] == head[None, :]).astype(np.float32) / NA_HEAD_DIM)


def _inproj(x2, mod3, n1, w_in_b, qn_row, kn_row, rows_per_batch):
    t = x2.shape[0]
    tm = 512
    per = rows_per_batch // tm
    row = lambda i: (i, 0)
    const = lambda i: (0, 0)
    wide = pl.BlockSpec((tm, D_MODEL), row)
    return pl.pallas_call(
        _inproj_kernel,
        grid=(t // tm,),
        in_specs=[wide,
                  pl.BlockSpec((None, 1, 6 * D_MODEL), lambda i: (i // per, 0, 0)),
                  _resident((1, D_MODEL), const),
                  _resident((D_MODEL, IN_COLS), const),
                  _resident((NA_WIDTH, NA_WIDTH), const),
                  _resident((1, NA_WIDTH), const),
                  _resident((1, NA_WIDTH), const)],
        out_specs=[pl.BlockSpec((tm, 3 * NA_WIDTH), row), wide, wide, wide, wide],
        out_shape=[jax.ShapeDtypeStruct((t, 3 * NA_WIDTH), BF16)] + [jax.ShapeDtypeStruct((t, D_MODEL), F32)] * 4,
        compiler_params=_cparams(("parallel",), 56),
        name="inproj",
    )(x2, mod3, n1, w_in_b, _head_mean_matrix(), qn_row, kn_row)


def _inproj_ctx(ctx2, mod3, n1, w_in_b, kn_row, ctx_mod_row):
    t = ctx2.shape[0]
    tm = 512
    row = lambda i: (i, 0)
    const = lambda i: (0, 0)
    wide = pl.BlockSpec((tm, D_MODEL), row)
    return pl.pallas_call(
        _inproj_ctx_kernel,
        grid=(t // tm,),
        in_specs=[wide,
                  pl.BlockSpec((None, 1, 6 * D_MODEL), lambda i: (ctx_mod_row, 0, 0)),
                  _resident((1, D_MODEL), const),
                  _resident((D_MODEL, IN_COLS), const),
                  _resident((NA_WIDTH, NA_WIDTH), const),
                  _resident((1, NA_WIDTH), const)],
        out_specs=[wide, wide],
        out_shape=[jax.ShapeDtypeStruct((t, 2 * NA_WIDTH), BF16), jax.ShapeDtypeStruct((t, D_MODEL), F32)],
        compiler_params=_cparams(("parallel",), 56),
        name="inproj_ctx",
    )(ctx2, mod3, n1, w_in_b, _head_mean_matrix(), kn_row)


def _attn_bias_table(rpb):
    wr, wc, w = NA_WIN_ROWS, NA_WIN_COLS, GRID_W
    col = np.arange(w)
    col0 = np.clip(col - wc // 2, 0, w - wc)
    col_mask = (col[None, :] >= col0[:, None]) & (col[None, :] < col0[:, None] + wc)
    dc = np.clip(col[None, :] - col[:, None], -(wc - 1), wc - 1) + wc - 1
    tab = jnp.where(col_mask[None, None], rpb[:, :, dc], NEG_INF)
    none = jnp.full((NA_HEADS, 1, w, w), NEG_INF, F32)
    return jnp.concatenate([jnp.concatenate([none, tab], axis=1),
                            jnp.concatenate([tab, none], axis=1)], axis=-1)


def _fill_bias(tab_ref, bias_ref, rb, rows):
    wr, w = NA_WIN_ROWS, GRID_W
    ks = int(np.clip(rb * ATT_QROWS - wr // 2, 0, rows - ATT_KROWS))
    lane = lax.broadcasted_iota(jnp.int32, (w, 2 * w), 1)
    for hh in range(V7X_LANES // NA_HEAD_DIM):
        for i in range(ATT_QROWS):
            r = rb * ATT_QROWS + i
            row0 = int(np.clip(r - wr // 2, 0, rows - wr))
            for w2 in range(ATT_KROWS // 2):
                kr = ks + 2 * w2
                left = row0 <= kr < row0 + wr
                right = row0 <= kr + 1 < row0 + wr
                if left or right:
                    blk = tab_ref[hh, kr - r + wr]
                    if not right:
                        blk = jnp.where(lane < w, blk, NEG_INF)
                    elif not left:
                        blk = jnp.where(lane >= w, blk, NEG_INF)
                else:
                    blk = jnp.full((w, 2 * w), NEG_INF, F32)
                bias_ref[hh, i * w:(i + 1) * w, w2 * 2 * w:(w2 + 1) * 2 * w] = blk


def _attn_kernel(q_ref, k_ref, v_ref, kc_ref, vc_ref, tab_ref, o_ref, bias_ref, *, rows):
    rb = pl.program_id(1)
    n_rb = rows // ATT_QROWS

    for rb_static in (0, 1, n_rb - 1):
        @pl.when(jnp.logical_and(pl.program_id(2) == 0, rb == rb_static))
        def _():
            _fill_bias(tab_ref, bias_ref, rb_static, rows)

    ks = jnp.clip(rb * ATT_QROWS - NA_WIN_ROWS // 2, 0, rows - ATT_KROWS) * GRID_W
    ks = pl.multiple_of(ks, GRID_W)
    nk = ATT_KROWS * GRID_W
    q = q_ref[...]
    kw = k_ref[pl.ds(ks, nk), :]
    vw = v_ref[pl.ds(ks, nk), :]
    kc = kc_ref[...]
    vc = vc_ref[...]
    lane = lax.broadcasted_iota(jnp.int32, (1, V7X_LANES), 1)
    acc = jnp.zeros(q.shape, F32)
    for hh in range(V7X_LANES // NA_HEAD_DIM):
        sel = (lane // NA_HEAD_DIM == hh).astype(BF16)
        qh = q * sel
        s = lax.dot_general(qh, kw, _NT, preferred_element_type=F32) + bias_ref[hh]
        sc = lax.dot_general(qh, kc, _NT, preferred_element_type=F32)
        mx = jnp.maximum(jnp.max(s, axis=-1, keepdims=True), jnp.max(sc, axis=-1, keepdims=True))
        p = jnp.exp(s - mx)
        pc = jnp.exp(sc - mx)
        den = jnp.sum(p, axis=-1, keepdims=True) + jnp.sum(pc, axis=-1, keepdims=True)
        o = (jnp.dot(p.astype(BF16), vw * sel, preferred_element_type=F32)
             + jnp.dot(pc.astype(BF16), vc * sel, preferred_element_type=F32))
        acc = acc + o / den
    o_ref[...] = acc.astype(BF16)


def _attention(qkv, kv_c, tab, batch, seq):
    rows = seq // GRID_W
    n_rb = rows // ATT_QROWS
    tq = ATT_QROWS * GRID_W
    pairs = NA_WIDTH // V7X_LANES
    kcol, vcol = K0 // V7X_LANES, V0 // V7X_LANES
    assert n_rb >= 3
    heads_per_step = V7X_LANES // NA_HEAD_DIM
    return pl.pallas_call(
        functools.partial(_attn_kernel, rows=rows),
        grid=(pairs, n_rb, batch),
        in_specs=[pl.BlockSpec((tq, V7X_LANES), lambda hp, rb, b: (b * n_rb + rb, hp)),
                  pl.BlockSpec((seq, V7X_LANES), lambda hp, rb, b: (b, kcol + hp)),
                  pl.BlockSpec((seq, V7X_LANES), lambda hp, rb, b: (b, vcol + hp)),
                  pl.BlockSpec((CTX_LEN, V7X_LANES), lambda hp, rb, b: (b, hp)),
                  pl.BlockSpec((CTX_LEN, V7X_LANES), lambda hp, rb, b: (b, pairs + hp)),
                  pl.BlockSpec((heads_per_step, 2 * NA_WIN_ROWS, GRID_W, 2 * GRID_W),
                               lambda hp, rb, b: (hp, 0, 0, 0))],
        out_specs=pl.BlockSpec((tq, V7X_LANES), lambda hp, rb, b: (b * n_rb + rb, hp)),
        out_shape=jax.ShapeDtypeStruct((batch * seq, NA_WIDTH), BF16),
        scratch_shapes=[pltpu.VMEM((heads_per_step, tq, ATT_KROWS * GRID_W), F32)],
        compiler_params=_cparams(("arbitrary", "arbitrary", "arbitrary"), 48),
        name="attn",
    )(qkv, qkv, qkv, kv_c, kv_c, tab)


def _lru_kernel(*refs, reverse, n_chunks):
    if reverse:
        (x_ref, xc_ref, prev_ref, next_ref, cw_ref, cb_ref, wg_ref, bg_ref, lam_ref,
         h_ref, win_ref, a_ref, b_ref, carry_ref) = refs
    else:
        (x_ref, xc_ref, prev_ref, next_ref, cw_ref, cb_ref, wg_ref, bg_ref, lam_ref, hb_ref, gy_ref,
         h_ref, win_ref, a_ref, b_ref, carry_ref) = refs
    c = pl.program_id(1)
    is_ctx = c == 0
    chunk = (n_chunks - c) if reverse else (c - 1)
    n = LRU_CHUNK

    has_prev = jnp.logical_and(c > 0, chunk > 0)
    has_next = jnp.logical_and(c > 0, chunk < n_chunks - 1)
    win_ref[0:HALO, :] = jnp.where(has_prev, prev_ref[...], 0.0)
    win_ref[HALO:HALO + n, :] = jnp.where(is_ctx, xc_ref[...], x_ref[...])
    win_ref[HALO + n:2 * HALO + n, :] = jnp.where(has_next, next_ref[...], 0.0)
    u = cb_ref[...]
    for j in range(CONV_WIDTH):
        off = HALO - CONV_LEFT + j
        u = u + cw_ref[j:j + 1, :] * win_ref[off:off + n, :]

    ub = u.astype(BF16)
    lam = lam_ref[...]
    softplus = jnp.maximum(-lam, 0.0) + jnp.log(1.0 + jnp.exp(-jnp.abs(lam)))
    for blk in range(LRU_BLOCKS):
        lo, hi = blk * LRU_BLOCK, (blk + 1) * LRU_BLOCK
        g = jnp.dot(ub[:, lo:hi], wg_ref[blk], preferred_element_type=F32)
        r = _sigmoid(g[:, 0:LRU_BLOCK] + bg_ref[0:1, lo:hi])
        i = _sigmoid(g[:, LRU_BLOCK:2 * LRU_BLOCK] + bg_ref[1:2, lo:hi])
        log_a = -LRU_C * r * softplus[:, lo:hi]
        a = jnp.exp(log_a)
        beta = jnp.sqrt(1.0 - a * a)
        a_ref[:, lo:hi] = a
        b_ref[:, lo:hi] = beta * (i * u[:, lo:hi])

    a = a_ref[...]
    b = b_ref[...]
    pos = lax.broadcasted_iota(jnp.int32, (n, 1), 0) % V7X_SUBLANES
    for s in (1, 2, 4):
        if reverse:
            a_n = pltpu.roll(a, n - s, axis=0)
            b_n = pltpu.roll(b, n - s, axis=0)
            ok = pos < V7X_SUBLANES - s
        else:
            a_n = pltpu.roll(a, s, axis=0)
            b_n = pltpu.roll(b, s, axis=0)
            ok = pos >= s
        b = jnp.where(ok, a * b_n + b, b)
        a = jnp.where(ok, a * a_n, a)
    a_ref[...] = a
    b_ref[...] = b

    @pl.when(is_ctx)
    def _():
        carry_ref[...] = jnp.zeros_like(carry_ref)

    groups = n // V7X_SUBLANES
    edge = 0 if reverse else V7X_SUBLANES - 1

    def group_step(gi, carry):
        g = (groups - 1 - gi) if reverse else gi
        row = pl.multiple_of(g * V7X_SUBLANES, V7X_SUBLANES)
        hg = a_ref[pl.ds(row, V7X_SUBLANES), :] * carry + b_ref[pl.ds(row, V7X_SUBLANES), :]
        if reverse:
            h_ref[pl.ds(row, V7X_SUBLANES), :] = hg
        else:
            h_ref[pl.ds(row, V7X_SUBLANES), :] = ((hg + hb_ref[pl.ds(row, V7X_SUBLANES), :])
                                                  * gy_ref[pl.ds(row, V7X_SUBLANES), :])
        return hg[edge:edge + 1, :]

    carry_ref[...] = lax.fori_loop(0, groups, group_step, carry_ref[...])


def _lru_pass(xr, xr_c, conv_w, conv_b, w_gate, b_gate, lam, batch, seq, reverse, h_bwd=None, gy=None):
    n = LRU_CHUNK
    n_chunks = seq // n
    per_halo = n // HALO

    def chunk_of(c):
        return jnp.where(c == 0, (n_chunks - 1) if reverse else 0, (n_chunks - c) if reverse else (c - 1))

    cur = lambda b, c: (b * n_chunks + chunk_of(c), 0)
    prev = lambda b, c: (jnp.maximum((b * n_chunks + chunk_of(c)) * per_halo - 1, 0), 0)
    nxt = lambda b, c: (jnp.minimum((b * n_chunks + chunk_of(c) + 1) * per_halo, batch * seq // HALO - 1), 0)
    const2 = lambda b, c: (0, 0)
    const3 = lambda b, c: (0, 0, 0)
    big = pl.BlockSpec((n, LRU_WIDTH), cur)
    in_specs = [big,
                pl.BlockSpec((n, LRU_WIDTH), lambda b, c: (b, 0)),
                pl.BlockSpec((HALO, LRU_WIDTH), prev),
                pl.BlockSpec((HALO, LRU_WIDTH), nxt),
                _resident((CONV_WIDTH, LRU_WIDTH), const2),
                _resident((1, LRU_WIDTH), const2),
                _resident((LRU_BLOCKS, LRU_BLOCK, 2 * LRU_BLOCK), const3),
                _resident((2, LRU_WIDTH), const2),
                _resident((1, LRU_WIDTH), const2)]
    args = [xr, xr_c, xr, xr, conv_w, conv_b, w_gate, b_gate, lam]
    if not reverse:
        in_specs += [big, big]
        args += [h_bwd, gy]
    return pl.pallas_call(
        functools.partial(_lru_kernel, reverse=reverse, n_chunks=n_chunks),
        grid=(batch, n_chunks + 1),
        in_specs=in_specs,
        out_specs=big,
        out_shape=jax.ShapeDtypeStruct((batch * seq, LRU_WIDTH), F32),
        scratch_shapes=[pltpu.VMEM((n + 2 * HALO, LRU_WIDTH), F32),
                        pltpu.VMEM((n, LRU_WIDTH), F32),
                        pltpu.VMEM((n, LRU_WIDTH), F32),
                        pltpu.VMEM((1, LRU_WIDTH), F32)],
        compiler_params=_cparams(("parallel", "arbitrary"), 40),
        name="lru_bwd" if reverse else "lru_fwd",
    )(*args)


def _merge_kernel(oa_ref, ob_ref, sga_ref, sgb_ref, x_ref, mod_ref, woa_ref, wol_ref, wout_ref,
                  n2_ref, wq_ref, keys_ref, x1_ref, h2_ref, st_ref):
    ma = jnp.dot(oa_ref[...], woa_ref[...], preferred_element_type=F32)
    mb = jnp.dot(ob_ref[...].astype(BF16), wol_ref[...], preferred_element_type=F32)
    merged = sga_ref[...] * ma + sgb_ref[...] * mb
    gate1 = mod_ref[:, 2 * D_MODEL:3 * D_MODEL]
    x1 = x_ref[...] + gate1 * jnp.dot(merged.astype(BF16), wout_ref[...], preferred_element_type=F32)
    x1_ref[...] = x1
    ms = jnp.mean(x1 * x1, axis=-1, keepdims=True)
    h = x1 * lax.rsqrt(ms + EPS) * n2_ref[...]
    shift2 = mod_ref[:, 3 * D_MODEL:4 * D_MODEL]
    scale2 = mod_ref[:, 4 * D_MODEL:5 * D_MODEL]
    h2 = (h * (1.0 + scale2) + shift2).astype(BF16)
    h2_ref[...] = h2
    qp = jnp.dot(h2, wq_ref[...], preferred_element_type=F32).astype(BF16)
    for hp in range(2 * PEER_HEADS):
        st_ref[hp] = lax.dot_general(keys_ref[hp], qp[:, hp * PEER_HALF:(hp + 1) * PEER_HALF], _NT,
                                     preferred_element_type=F32)


def _merge(o_a, o_b, sga, sgb, x2, mod3, w_oa, w_ol, w_out, n2, w_q, keys, rows_per_batch):
    t = x2.shape[0]
    tm = 512
    per = rows_per_batch // tm
    row = lambda i: (i, 0)
    const = lambda i: (0, 0)
    wide = pl.BlockSpec((tm, D_MODEL), row)
    nq = PEER_HEADS * PEER_QDIM
    return pl.pallas_call(
        _merge_kernel,
        grid=(t // tm,),
        in_specs=[pl.BlockSpec((tm, NA_WIDTH), row), wide, wide, wide, wide,
                  pl.BlockSpec((None, 1, 6 * D_MODEL), lambda i: (i // per, 0, 0)),
                  _resident((NA_WIDTH, D_MODEL), const),
                  _resident((LRU_WIDTH, D_MODEL), const),
                  _resident((D_MODEL, D_MODEL), const),
                  _resident((1, D_MODEL), const),
                  _resident((D_MODEL, nq), const),
                  _resident((2 * PEER_HEADS, PEER_NKEYS, PEER_HALF), lambda i: (0, 0, 0))],
        out_specs=[wide, wide, pl.BlockSpec((2 * PEER_HEADS, PEER_NKEYS, tm), lambda i: (0, 0, i))],
        out_shape=[jax.ShapeDtypeStruct((t, D_MODEL), F32),
                   jax.ShapeDtypeStruct((t, D_MODEL), BF16),
                   jax.ShapeDtypeStruct((2 * PEER_HEADS, PEER_NKEYS, t), F32)],
        compiler_params=_cparams(("parallel",), 56),
        name="merge",
    )(o_a, o_b, sga, sgb, x2, mod3, w_oa, w_ol, w_out, n2, w_q, keys)


def _top_ranked(x, k):
    rank = jnp.full(x.shape, float(k), F32)
    vals = []
    for b in range(k):
        m = jnp.max(x, axis=0, keepdims=True)
        hit = x >= m
        rank = jnp.where(hit, float(b), rank)
        x = jnp.where(hit, NEG_INF, x)
        vals.append(m)
    return vals, rank


def _stack_rows(rows):
    cols = rows[0].shape[1]
    slot = lax.broadcasted_iota(jnp.int32, (len(rows), cols), 0)
    out = jnp.zeros((len(rows), cols), F32)
    for b, row in enumerate(rows):
        out = jnp.where(slot == b, row, out)
    return out


_PAIR_LIMIT = [PEER_TOPK // (a + 1) for a in range(PEER_TOPK)]
_PAIR_HALF = PEER_TOPK // 2


def _select_kernel(st_ref, c1_ref, p1_ref, r2_ref, p2_ref):
    k = PEER_TOPK
    cols = st_ref.shape[2]
    slot = lax.broadcasted_iota(jnp.int32, (_PAIR_HALF, cols), 0)
    for h in range(PEER_HEADS):
        s1 = st_ref[2 * h]
        s2 = st_ref[2 * h + 1]
        v1, rank1 = _top_ranked(s1, k)
        v2, rank2 = _top_ranked(s2, k)
        v2_all = _stack_rows(v2)
        groups = [v1[0] + v2_all]
        for a in range(1, _PAIR_HALF):
            groups.append(jnp.where(slot < _PAIR_LIMIT[a], v1[a] + v2_all[0:_PAIR_HALF], NEG_INF))
        groups.append(_stack_rows(v1[_PAIR_HALF:]) + v2[0])
        cand0 = jnp.concatenate(groups, axis=0)
        cand = cand0
        tau = jnp.full_like(v1[0], NEG_INF)
        seen = jnp.zeros_like(v1[0])
        for _ in range(k):
            m = jnp.max(cand, axis=0, keepdims=True)
            hit = cand >= m
            tau = jnp.where(seen < k, m, tau)
            seen = seen + jnp.sum(hit.astype(F32), axis=0, keepdims=True)
            cand = jnp.where(hit, NEG_INF, cand)
        picked = cand0 >= tau
        z = jnp.sum(jnp.where(picked, jnp.exp(cand0 - (v1[0] + v2[0])), 0.0), axis=0, keepdims=True)
        pf = picked.astype(F32)
        counts = [jnp.sum(pf[0:k], axis=0, keepdims=True)]
        for a in range(1, _PAIR_HALF):
            lo = k + (a - 1) * _PAIR_HALF
            counts.append(jnp.sum(pf[lo:lo + _PAIR_HALF], axis=0, keepdims=True))
        lo = k + (_PAIR_HALF - 1) * _PAIR_HALF
        counts += [pf[lo + a:lo + a + 1] for a in range(k - _PAIR_HALF)]
        c1 = jnp.zeros_like(s1)
        for a in range(k):
            c1 = jnp.where(rank1 == float(a), counts[a], c1)
        c1_ref[h] = c1
        p1_ref[h] = jnp.exp(s1 - v1[0])
        r2_ref[h * PEER_NKEYS:(h + 1) * PEER_NKEYS, :] = rank2.astype(BF16)
        p2_ref[h * PEER_NKEYS:(h + 1) * PEER_NKEYS, :] = (jnp.exp(s2 - v2[0]) / z).astype(BF16)


def _select(st):
    t = st.shape[2]
    tt = 256
    blk = pl.BlockSpec((PEER_HEADS, PEER_NKEYS, tt), lambda i: (0, 0, i))
    flat = pl.BlockSpec((PEER_HEADS * PEER_NKEYS, tt), lambda i: (0, i))
    full = jax.ShapeDtypeStruct((PEER_HEADS, PEER_NKEYS, t), F32)
    full_flat = jax.ShapeDtypeStruct((PEER_HEADS * PEER_NKEYS, t), BF16)
    return pl.pallas_call(
        _select_kernel,
        grid=(t // tt,),
        in_specs=[pl.BlockSpec((2 * PEER_HEADS, PEER_NKEYS, tt), lambda i: (0, 0, i))],
        out_specs=[blk, blk, flat, flat],
        out_shape=[full, full, full_flat, full_flat],
        compiler_params=_cparams(("parallel",), 40),
        name="select",
    )(st)


EXP_TOK = 512
EXP_BLK = 1024
EXP_I = EXP_BLK // PEER_NKEYS
GATE_ROWS = V7X_SUBLANES
GATE_LANES = 2 * V7X_LANES


def _gate_times_act(at_ref, m_ref, c1_ref, p1_ref, r2_ref, p2_ref):
    jt, jl = GATE_ROWS, GATE_LANES
    for lt in range(EXP_TOK // jl):
        lanes = slice(lt * jl, (lt + 1) * jl)
        c1_rows = [c1_ref[h, :, lanes] for h in range(PEER_HEADS)]
        p1_rows = [p1_ref[h, :, lanes] for h in range(PEER_HEADS)]
        for il in range(EXP_I):
            w = [None] * (PEER_NKEYS // jt)
            for h in range(PEER_HEADS):
                c1 = jnp.broadcast_to(c1_rows[h][il:il + 1, :], (jt, jl)).astype(BF16)
                p1 = jnp.broadcast_to(p1_rows[h][il:il + 1, :], (jt, jl)).astype(BF16)
                for jb in range(PEER_NKEYS // jt):
                    js = slice(h * PEER_NKEYS + jb * jt, h * PEER_NKEYS + (jb + 1) * jt)
                    term = jnp.where(r2_ref[js, lanes] < c1, p2_ref[js, lanes] * p1, 0)
                    w[jb] = term if h == 0 else w[jb] + term
            row0 = il * PEER_NKEYS
            m_ref[row0:row0 + PEER_NKEYS, lanes] = _gelu_bf16(at_ref[row0:row0 + PEER_NKEYS, lanes])
            for jb in range(PEER_NKEYS // jt):
                ms = slice(row0 + jb * jt, row0 + (jb + 1) * jt)
                m_ref[ms, lanes] = w[jb] * m_ref[ms, lanes]


def _experts_kernel(h2_ref, u_ref, vt_ref, c1_ref, p1_ref, r2_ref, p2_ref, x1_ref, mod_ref,
                    o_ref, at_ref, m_ref, yt_ref):
    eb = pl.program_id(1)

    @pl.when(eb == 0)
    def _():
        yt_ref[...] = jnp.zeros_like(yt_ref)

    at_ref[...] = lax.dot_general(u_ref[...], h2_ref[...], _NT, preferred_element_type=F32)
    _gate_times_act(at_ref, m_ref, c1_ref, p1_ref, r2_ref, p2_ref)
    yt_ref[...] += jnp.dot(vt_ref[...], m_ref[...], preferred_element_type=F32)

    @pl.when(eb == pl.num_programs(1) - 1)
    def _():
        gate2 = mod_ref[:, 5 * D_MODEL:6 * D_MODEL]
        o_ref[...] = x1_ref[...] + gate2 * yt_ref[...].T


def _experts(h2, u_b, vt_b, c1, p1, r2, p2, x1, mod3, rows_per_batch):
    t = h2.shape[0]
    per = rows_per_batch // EXP_TOK
    tok = lambda i, e: (i, 0)
    rows = pl.BlockSpec((PEER_HEADS, EXP_I, EXP_TOK), lambda i, e: (0, e, i))
    sel_flat = pl.BlockSpec((PEER_HEADS * PEER_NKEYS, EXP_TOK), lambda i, e: (0, i))
    return pl.pallas_call(
        _experts_kernel,
        grid=(t // EXP_TOK, PEER_EXPERTS // EXP_BLK),
        in_specs=[pl.BlockSpec((EXP_TOK, D_MODEL), tok),
                  pl.BlockSpec((EXP_BLK, D_MODEL), lambda i, e: (e, 0)),
                  pl.BlockSpec((D_MODEL, EXP_BLK), lambda i, e: (0, e)),
                  rows, rows, sel_flat, sel_flat,
                  pl.BlockSpec((EXP_TOK, D_MODEL), tok),
                  pl.BlockSpec((None, 1, 6 * D_MODEL), lambda i, e: (i // per, 0, 0))],
        out_specs=pl.BlockSpec((EXP_TOK, D_MODEL), tok),
        out_shape=jax.ShapeDtypeStruct((t, D_MODEL), F32),
        scratch_shapes=[pltpu.VMEM((EXP_BLK, EXP_TOK), F32),
                        pltpu.VMEM((EXP_BLK, EXP_TOK), BF16),
                        pltpu.VMEM((D_MODEL, EXP_TOK), F32)],
        compiler_params=_cparams(("parallel", "arbitrary"), 48),
        name="experts",
    )(h2, u_b, vt_b, c1, p1, r2, p2, x1, mod3)


def _layer(x, c, ctx, c_ctx, w_ada, b_ada, norm1_w, norm2_w, w_in, q_norm_w, k_norm_w, na_rpb,
           conv_w, conv_b, lru_w_a, lru_b_a, lru_w_x, lru_b_x, lru_lambda, w_o_attn, w_o_lru,
           w_out, peer_w_q, peer_keys, peer_u, peer_v):
    batch, seq, d = x.shape
    n_ctx = ctx.shape[1]
    assert d == D_MODEL and n_ctx == CTX_LEN and seq % (ATT_QROWS * GRID_W) == 0 and batch < MOD_ROWS
    x2 = x.reshape(batch * seq, d)
    ctx2 = ctx.reshape(batch * n_ctx, d)

    c_rows = jnp.zeros((MOD_ROWS, d), F32).at[:batch].set(c).at[batch].set(c_ctx)
    mod3 = _mod(c_rows, w_ada, b_ada).reshape(MOD_ROWS, 1, 6 * d)

    w_in_b = w_in.astype(BF16)
    n1 = norm1_w.reshape(1, d)
    qn_row = (jnp.tile(q_norm_w, NA_HEADS) * (NA_HEAD_DIM ** -0.5)).reshape(1, NA_WIDTH)
    kn_row = jnp.tile(k_norm_w, NA_HEADS).reshape(1, NA_WIDTH)
    qkv, xr, gy, sga, sgb = _inproj(x2, mod3, n1, w_in_b, qn_row, kn_row, seq)
    kv_c, xr_c = _inproj_ctx(ctx2, mod3, n1, w_in_b, kn_row, batch)

    o_a = _attention(qkv, kv_c, _attn_bias_table(na_rpb), batch, seq)

    cb = conv_b.reshape(1, LRU_WIDTH)
    outs = []
    h_bwd = None
    for direction in (1, 0):
        w_gate = jnp.concatenate([lru_w_a[direction], lru_w_x[direction]], axis=-1).astype(BF16)
        b_gate = jnp.stack([lru_b_a[direction], lru_b_x[direction]])
        lam = lru_lambda[direction].reshape(1, LRU_WIDTH)
        if direction == 1:
            h_bwd = _lru_pass(xr, xr_c, conv_w, cb, w_gate, b_gate, lam, batch, seq, True)
        else:
            o_b = _lru_pass(xr, xr_c, conv_w, cb, w_gate, b_gate, lam, batch, seq, False, h_bwd, gy)

    keys = peer_keys.reshape(2 * PEER_HEADS, PEER_NKEYS, PEER_HALF).astype(BF16)
    x1, h2, st = _merge(o_a, o_b, sga, sgb, x2, mod3, w_o_attn.astype(BF16), w_o_lru.astype(BF16),
                        w_out.astype(BF16), norm2_w.reshape(1, d), peer_w_q.astype(BF16), keys, seq)
    c1, p1, r2, p2 = _select(st)
    out = _experts(h2, peer_u.astype(BF16), peer_v.astype(BF16).T, c1, p1, r2, p2, x1, mod3, seq)
    return out.reshape(batch, seq, d)


def kernel(x, c, ctx, c_ctx, w_ada, b_ada, norm1_w, norm2_w, w_in, q_norm_w, k_norm_w, na_rpb, conv_w, conv_b, lru_w_a, lru_b_a, lru_w_x, lru_b_x, lru_lambda, w_o_attn, w_o_lru, w_out, peer_w_q, peer_keys, peer_u, peer_v):
    depth = w_ada.shape[0]
    for layer in range(depth):
        x = _layer(x, c, ctx, c_ctx, w_ada[layer], b_ada[layer], norm1_w[layer], norm2_w[layer],
                   w_in[layer], q_norm_w[layer], k_norm_w[layer], na_rpb[layer], conv_w[layer],
                   conv_b[layer], lru_w_a[layer], lru_b_a[layer], lru_w_x[layer], lru_b_x[layer],
                   lru_lambda[layer], w_o_attn[layer], w_o_lru[layer], w_out[layer], peer_w_q[layer],
                   peer_keys[layer], peer_u[layer], peer_v[layer])
    return x
```

```python
import functools

import numpy as np
import jax
import jax.numpy as jnp
from jax import lax
from jax.experimental import pallas as pl
from jax.experimental.pallas import tpu as pltpu

F32 = jnp.float32
BF16 = jnp.bfloat16

D_MODEL = 1024
GRID_W = 64
CTX_LEN = 256
NA_HEADS = 8
NA_HEAD_DIM = 64
NA_WIDTH = NA_HEADS * NA_HEAD_DIM
NA_WIN_ROWS = 8
NA_WIN_COLS = 16
LRU_WIDTH = D_MODEL
LRU_BLOCKS = 8
LRU_BLOCK = LRU_WIDTH // LRU_BLOCKS
CONV_WIDTH = 4
CONV_LEFT = 2
LRU_C = 8.0
PEER_HEADS = 8
PEER_NKEYS = 128
PEER_EXPERTS = PEER_NKEYS * PEER_NKEYS
PEER_QDIM = 256
PEER_HALF = PEER_QDIM // 2
PEER_TOPK = 16
EPS = 1e-6
NEG_INF = -1e30

Q0 = 0
K0 = Q0 + NA_WIDTH
V0 = K0 + NA_WIDTH
XR0 = V0 + NA_WIDTH
YR0 = XR0 + LRU_WIDTH
GA0 = YR0 + LRU_WIDTH
GB0 = GA0 + D_MODEL
IN_COLS = GB0 + D_MODEL

V7X_LANES = 128
V7X_SUBLANES = 8
BF16_SUBLANES = 2 * V7X_SUBLANES
V7X_VMEM_BYTES = 64 * 1024 * 1024

MOD_ROWS = 8
ATT_QROWS = 8
ATT_KROWS = 16
LRU_CHUNK = 256
HALO = V7X_SUBLANES

_NT = (((1,), (1,)), ((), ()))


def _cparams(sem, vmem_mb, flags=None):
    return pltpu.CompilerParams(dimension_semantics=sem, vmem_limit_bytes=vmem_mb * 1024 * 1024, flags=flags)


def _resident(shape, index_map):
    return pl.BlockSpec(shape, index_map, pipeline_mode=pl.Buffered(1))


def _gelu(x):
    return jax.nn.gelu(x)


def _sigmoid(x):
    return jax.nn.sigmoid(x)


_GELU_A = -2.0 * float(np.sqrt(2.0 / np.pi) * np.log2(np.e))
_GELU_B = _GELU_A * 0.044715


def _gelu_bf16(x):
    xb = x.astype(BF16)
    z = (xb * xb * _GELU_B + _GELU_A) * xb
    return xb / (1.0 + jnp.exp2(z))


def _mod_kernel(c_ref, w_ref, b_ref, o_ref):
    cs = c_ref[...]
    s = cs * _sigmoid(cs)
    o_ref[...] = jnp.dot(s, w_ref[...], preferred_element_type=F32,
                         precision=lax.Precision.HIGHEST) + b_ref[...]


def _mod(c_rows, w_ada, b_ada):
    n = w_ada.shape[1]
    tn = 1536
    return pl.pallas_call(
        _mod_kernel,
        grid=(n // tn,),
        in_specs=[pl.BlockSpec((MOD_ROWS, D_MODEL), lambda j: (0, 0)),
                  pl.BlockSpec((D_MODEL, tn), lambda j: (0, j)),
                  pl.BlockSpec((1, tn), lambda j: (0, j))],
        out_specs=pl.BlockSpec((MOD_ROWS, tn), lambda j: (0, j)),
        out_shape=jax.ShapeDtypeStruct((MOD_ROWS, n), F32),
        compiler_params=_cparams(("arbitrary",), 40),
        name="mod",
    )(c_rows, w_ada, b_ada.reshape(1, n))


def _normed_input(x_ref, mod_ref, n1_ref):
    x = x_ref[...]
    ms = jnp.mean(x * x, axis=-1, keepdims=True)
    h = x * lax.rsqrt(ms + EPS) * n1_ref[...]
    shift = mod_ref[:, 0:D_MODEL]
    scale = mod_ref[:, D_MODEL:2 * D_MODEL]
    return (h * (1.0 + scale) + shift).astype(BF16)


def _head_norm(z, g_ref, w_row):
    zz = z * z
    hi = zz.astype(BF16)
    lo = (zz - hi.astype(F32)).astype(BF16)
    ms = (jnp.dot(hi, g_ref[...], preferred_element_type=F32)
          + jnp.dot(lo, g_ref[...], preferred_element_type=F32))
    return z * lax.rsqrt(ms + EPS) * w_row


def _inproj_kernel(x_ref, mod_ref, n1_ref, w_ref, g_ref, qn_ref, kn_ref,
                   qkv_ref, xr_ref, gy_ref, sga_ref, sgb_ref):
    hb = _normed_input(x_ref, mod_ref, n1_ref)
    seg = lambda lo, hi: jnp.dot(hb, w_ref[:, lo:hi], preferred_element_type=F32)
    qkv_ref[:, Q0:K0] = _head_norm(seg(Q0, K0), g_ref, qn_ref[...]).astype(BF16)
    qkv_ref[:, K0:V0] = _head_norm(seg(K0, V0), g_ref, kn_ref[...]).astype(BF16)
    qkv_ref[:, V0:XR0] = seg(V0, XR0).astype(BF16)
    xr_ref[...] = seg(XR0, YR0)
    gy_ref[...] = _gelu(seg(YR0, GA0))
    sga_ref[...] = _sigmoid(seg(GA0, GB0))
    sgb_ref[...] = _sigmoid(seg(GB0, IN_COLS))


def _inproj_ctx_kernel(x_ref, mod_ref, n1_ref, w_ref, g_ref, kn_ref, kv_ref, xr_ref):
    hb = _normed_input(x_ref, mod_ref, n1_ref)
    seg = lambda lo, hi: jnp.dot(hb, w_ref[:, lo:hi], preferred_element_type=F32)
    kv_ref[:, 0:NA_WIDTH] = _head_norm(seg(K0, V0), g_ref, kn_ref[...]).astype(BF16)
    kv_ref[:, NA_WIDTH:2 * NA_WIDTH] = seg(V0, XR0).astype(BF16)
    xr_ref[...] = seg(XR0, YR0)


def _head_mean_matrix():
    head = np.arange(NA_WIDTH) // NA_HEAD_DIM
    return jnp.asarray((head[:, None] == head[None, :]).astype(np.float32) / NA_HEAD_DIM, dtype=BF16)


def _inproj(x2, mod3, n1, w_in_b, qn_row, kn_row, rows_per_batch):
    t = x2.shape[0]
    tm = 512
    per = rows_per_batch // tm
    row = lambda i: (i, 0)
    const = lambda i: (0, 0)
    wide = pl.BlockSpec((tm, D_MODEL), row)
    return pl.pallas_call(
        _inproj_kernel,
        grid=(t // tm,),
        in_specs=[wide,
                  pl.BlockSpec((None, 1, 6 * D_MODEL), lambda i: (i // per, 0, 0)),
                  _resident((1, D_MODEL), const),
                  _resident((D_MODEL, IN_COLS), const),
                  _resident((NA_WIDTH, NA_WIDTH), const),
                  _resident((1, NA_WIDTH), const),
                  _resident((1, NA_WIDTH), const)],
        out_specs=[pl.BlockSpec((tm, 3 * NA_WIDTH), row), wide, wide, wide, wide],
        out_shape=[jax.ShapeDtypeStruct((t, 3 * NA_WIDTH), BF16)] + [jax.ShapeDtypeStruct((t, D_MODEL), F32)] * 4,
        compiler_params=_cparams(("parallel",), 56),
        name="inproj",
    )(x2, mod3, n1, w_in_b, _head_mean_matrix(), qn_row, kn_row)


def _inproj_ctx(ctx2, mod3, n1, w_in_b, kn_row, ctx_mod_row):
    t = ctx2.shape[0]
    tm = 512
    row = lambda i: (i, 0)
    const = lambda i: (0, 0)
    wide = pl.BlockSpec((tm, D_MODEL), row)
    return pl.pallas_call(
        _inproj_ctx_kernel,
        grid=(t // tm,),
        in_specs=[wide,
                  pl.BlockSpec((None, 1, 6 * D_MODEL), lambda i: (ctx_mod_row, 0, 0)),
                  _resident((1, D_MODEL), const),
                  _resident((D_MODEL, IN_COLS), const),
                  _resident((NA_WIDTH, NA_WIDTH), const),
                  _resident((1, NA_WIDTH), const)],
        out_specs=[wide, wide],
        out_shape=[jax.ShapeDtypeStruct((t, 2 * NA_WIDTH), BF16), jax.ShapeDtypeStruct((t, D_MODEL), F32)],
        compiler_params=_cparams(("parallel",), 56),
        name="inproj_ctx",
    )(ctx2, mod3, n1, w_in_b, _head_mean_matrix(), kn_row)


def _attn_bias_table(rpb):
    wr, wc, w = NA_WIN_ROWS, NA_WIN_COLS, GRID_W
    col = np.arange(w)
    col0 = np.clip(col - wc // 2, 0, w - wc)
    col_mask = (col[None, :] >= col0[:, None]) & (col[None, :] < col0[:, None] + wc)
    dc = np.clip(col[None, :] - col[:, None], -(wc - 1), wc - 1) + wc - 1
    tab = jnp.where(col_mask[None, None], rpb[:, :, dc], NEG_INF)
    none = jnp.full((NA_HEADS, 1, w, w), NEG_INF, F32)
    return jnp.concatenate([jnp.concatenate([none, tab], axis=1),
                            jnp.concatenate([tab, none], axis=1)], axis=-1)


def _fill_bias(tab_ref, bias_ref, rb, rows):
    wr, w = NA_WIN_ROWS, GRID_W
    ks = int(np.clip(rb * ATT_QROWS - wr // 2, 0, rows - ATT_KROWS))
    lane = lax.broadcasted_iota(jnp.int32, (w, 2 * w), 1)
    for hh in range(V7X_LANES // NA_HEAD_DIM):
        for i in range(ATT_QROWS):
            r = rb * ATT_QROWS + i
            row0 = int(np.clip(r - wr // 2, 0, rows - wr))
            for w2 in range(ATT_KROWS // 2):
                kr = ks + 2 * w2
                left = row0 <= kr < row0 + wr
                right = row0 <= kr + 1 < row0 + wr
                if left or right:
                    blk = tab_ref[hh, kr - r + wr]
                    if not right:
                        blk = jnp.where(lane < w, blk, NEG_INF)
                    elif not left:
                        blk = jnp.where(lane >= w, blk, NEG_INF)
                else:
                    blk = jnp.full((w, 2 * w), NEG_INF, F32)
                bias_ref[hh, i * w:(i + 1) * w, w2 * 2 * w:(w2 + 1) * 2 * w] = blk


def _attn_kernel(q_ref, k_ref, v_ref, kc_ref, vc_ref, tab_ref, o_ref, bias_ref, *, rows):
    rb = pl.program_id(1)
    n_rb = rows // ATT_QROWS

    for rb_static in (0, 1, n_rb - 1):
        @pl.when(jnp.logical_and(pl.program_id(2) == 0, rb == rb_static))
        def _():
            _fill_bias(tab_ref, bias_ref, rb_static, rows)

    ks = jnp.clip(rb * ATT_QROWS - NA_WIN_ROWS // 2, 0, rows - ATT_KROWS) * GRID_W
    ks = pl.multiple_of(ks, GRID_W)
    nk = ATT_KROWS * GRID_W
    q = q_ref[...]
    kw = k_ref[pl.ds(ks, nk), :]
    vw = v_ref[pl.ds(ks, nk), :]
    kc = kc_ref[...]
    vc = vc_ref[...]
    lane = lax.broadcasted_iota(jnp.int32, (1, V7X_LANES), 1)
    acc = jnp.zeros(q.shape, F32)
    for hh in range(V7X_LANES // NA_HEAD_DIM):
        sel = (lane // NA_HEAD_DIM == hh).astype(BF16)
        qh = q * sel
        s = lax.dot_general(qh, kw, _NT, preferred_element_type=F32) + bias_ref[hh]
        sc = lax.dot_general(qh, kc, _NT, preferred_element_type=F32)
        mx = jnp.maximum(jnp.max(s, axis=-1, keepdims=True), jnp.max(sc, axis=-1, keepdims=True))
        p = jnp.exp(s - mx)
        pc = jnp.exp(sc - mx)
        den = jnp.sum(p, axis=-1, keepdims=True) + jnp.sum(pc, axis=-1, keepdims=True)
        o = (jnp.dot(p.astype(BF16), vw * sel, preferred_element_type=F32)
             + jnp.dot(pc.astype(BF16), vc * sel, preferred_element_type=F32))
        acc = acc + o / den
    o_ref[...] = acc.astype(BF16)


def _attention(qkv, kv_c, tab, batch, seq):
    rows = seq // GRID_W
    n_rb = rows // ATT_QROWS
    tq = ATT_QROWS * GRID_W
    pairs = NA_WIDTH // V7X_LANES
    kcol, vcol = K0 // V7X_LANES, V0 // V7X_LANES
    assert n_rb >= 3
    heads_per_step = V7X_LANES // NA_HEAD_DIM
    return pl.pallas_call(
        functools.partial(_attn_kernel, rows=rows),
        grid=(pairs, n_rb, batch),
        in_specs=[pl.BlockSpec((tq, V7X_LANES), lambda hp, rb, b: (b * n_rb + rb, hp)),
                  pl.BlockSpec((seq, V7X_LANES), lambda hp, rb, b: (b, kcol + hp)),
                  pl.BlockSpec((seq, V7X_LANES), lambda hp, rb, b: (b, vcol + hp)),
                  pl.BlockSpec((CTX_LEN, V7X_LANES), lambda hp, rb, b: (b, hp)),
                  pl.BlockSpec((CTX_LEN, V7X_LANES), lambda hp, rb, b: (b, pairs + hp)),
                  pl.BlockSpec((heads_per_step, 2 * NA_WIN_ROWS, GRID_W, 2 * GRID_W),
                               lambda hp, rb, b: (hp, 0, 0, 0))],
        out_specs=pl.BlockSpec((tq, V7X_LANES), lambda hp, rb, b: (b * n_rb + rb, hp)),
        out_shape=jax.ShapeDtypeStruct((batch * seq, NA_WIDTH), BF16),
        scratch_shapes=[pltpu.VMEM((heads_per_step, tq, ATT_KROWS * GRID_W), F32)],
        compiler_params=_cparams(("arbitrary", "arbitrary", "arbitrary"), 48),
        name="attn",
    )(qkv, qkv, qkv, kv_c, kv_c, tab)


def _lru_kernel(*refs, reverse, n_chunks):
    if reverse:
        (x_ref, xc_ref, prev_ref, next_ref, cw_ref, cb_ref, wg_ref, bg_ref, lam_ref,
         h_ref, u_ref, uc_ref, win_ref, a_ref, b_ref, carry_ref) = refs
    else:
        (u_ref, uc_ref, wg_ref, bg_ref, lam_ref, hb_ref, gy_ref, h_ref, a_ref, b_ref, carry_ref) = refs
    c = pl.program_id(1)
    is_ctx = c == 0
    n = LRU_CHUNK

    if reverse:
        chunk = n_chunks - c
        has_prev = jnp.logical_and(c > 0, chunk > 0)
        has_next = jnp.logical_and(c > 0, chunk < n_chunks - 1)
        win_ref[0:HALO, :] = jnp.where(has_prev, prev_ref[...], 0.0)
        win_ref[HALO:HALO + n, :] = jnp.where(is_ctx, xc_ref[...], x_ref[...])
        win_ref[HALO + n:2 * HALO + n, :] = jnp.where(has_next, next_ref[...], 0.0)
        u = cb_ref[...]
        for j in range(CONV_WIDTH):
            off = HALO - CONV_LEFT + j
            u = u + cw_ref[j:j + 1, :] * win_ref[off:off + n, :]
        u_ref[...] = u

        @pl.when(is_ctx)
        def _():
            uc_ref[...] = u
    else:
        u = jnp.where(is_ctx, uc_ref[...], u_ref[...])

    ub = u.astype(BF16)
    lam = lam_ref[...]
    softplus = jnp.maximum(-lam, 0.0) + jnp.log(1.0 + jnp.exp(-jnp.abs(lam)))
    for blk in range(LRU_BLOCKS):
        lo, hi = blk * LRU_BLOCK, (blk + 1) * LRU_BLOCK
        g = jnp.dot(ub[:, lo:hi], wg_ref[blk], preferred_element_type=F32)
        r = _sigmoid(g[:, 0:LRU_BLOCK] + bg_ref[0:1, lo:hi])
        i = _sigmoid(g[:, LRU_BLOCK:2 * LRU_BLOCK] + bg_ref[1:2, lo:hi])
        log_a = -LRU_C * r * softplus[:, lo:hi]
        a = jnp.exp(log_a)
        beta = jnp.sqrt(1.0 - a * a)
        a_ref[:, lo:hi] = a
        b_ref[:, lo:hi] = beta * (i * u[:, lo:hi])

    groups = n // V7X_SUBLANES
    a = a_ref[...].reshape(groups, V7X_SUBLANES, LRU_WIDTH)
    b = b_ref[...].reshape(groups, V7X_SUBLANES, LRU_WIDTH)
    pos = lax.broadcasted_iota(jnp.int32, (1, V7X_SUBLANES, 1), 1)
    for s in (1, 2, 4):
        shift = (V7X_SUBLANES - s) if reverse else s
        a_n = pltpu.roll(a, shift, axis=1)
        b_n = pltpu.roll(b, shift, axis=1)
        ok = (pos < V7X_SUBLANES - s) if reverse else (pos >= s)
        b = jnp.where(ok, a * b_n + b, b)
        a = jnp.where(ok, a * a_n, a)
    a_ref[...] = a.reshape(n, LRU_WIDTH)
    b_ref[...] = b.reshape(n, LRU_WIDTH)

    @pl.when(is_ctx)
    def _():
        carry_ref[...] = jnp.zeros_like(carry_ref)

    edge = 0 if reverse else V7X_SUBLANES - 1

    def group_step(gi, carry):
        g = (groups - 1 - gi) if reverse else gi
        row = pl.multiple_of(g * V7X_SUBLANES, V7X_SUBLANES)
        hg = a_ref[pl.ds(row, V7X_SUBLANES), :] * carry + b_ref[pl.ds(row, V7X_SUBLANES), :]
        if reverse:
            h_ref[pl.ds(row, V7X_SUBLANES), :] = hg
        else:
            h_ref[pl.ds(row, V7X_SUBLANES), :] = ((hg + hb_ref[pl.ds(row, V7X_SUBLANES), :])
                                                  * gy_ref[pl.ds(row, V7X_SUBLANES), :])
        return hg[edge:edge + 1, :]

    carry_ref[...] = lax.fori_loop(0, groups, group_step, carry_ref[...])


def _lru_pass(x_in, xc_in, w_gate, b_gate, lam, batch, seq, reverse, conv=None, h_bwd=None, gy=None):
    n = LRU_CHUNK
    n_chunks = seq // n
    per_halo = n // HALO

    def chunk_of(c):
        return jnp.where(c == 0, (n_chunks - 1) if reverse else 0, (n_chunks - c) if reverse else (c - 1))

    cur = lambda b, c: (b * n_chunks + chunk_of(c), 0)
    prev = lambda b, c: (jnp.maximum((b * n_chunks + chunk_of(c)) * per_halo - 1, 0), 0)
    nxt = lambda b, c: (jnp.minimum((b * n_chunks + chunk_of(c) + 1) * per_halo, batch * seq // HALO - 1), 0)
    const2 = lambda b, c: (0, 0)
    const3 = lambda b, c: (0, 0, 0)
    big = pl.BlockSpec((n, LRU_WIDTH), cur)
    ctx_blk = pl.BlockSpec((n, LRU_WIDTH), lambda b, c: (b, 0))
    gates = [_resident((LRU_BLOCKS, LRU_BLOCK, 2 * LRU_BLOCK), const3),
             _resident((2, LRU_WIDTH), const2),
             _resident((1, LRU_WIDTH), const2)]
    wide = jax.ShapeDtypeStruct((batch * seq, LRU_WIDTH), F32)
    scan_scratch = [pltpu.VMEM((n, LRU_WIDTH), F32), pltpu.VMEM((n, LRU_WIDTH), F32), pltpu.VMEM((1, LRU_WIDTH), F32)]
    if reverse:
        conv_w, conv_b = conv
        in_specs = [big, ctx_blk, pl.BlockSpec((HALO, LRU_WIDTH), prev), pl.BlockSpec((HALO, LRU_WIDTH), nxt),
                    _resident((CONV_WIDTH, LRU_WIDTH), const2), _resident((1, LRU_WIDTH), const2)] + gates
        args = [x_in, xc_in, x_in, x_in, conv_w, conv_b, w_gate, b_gate, lam]
        out_specs = [big, big, ctx_blk]
        out_shape = [wide, wide, jax.ShapeDtypeStruct(xc_in.shape, F32)]
        scratch = [pltpu.VMEM((n + 2 * HALO, LRU_WIDTH), F32)] + scan_scratch
    else:
        in_specs = [big, ctx_blk] + gates + [big, big]
        args = [x_in, xc_in, w_gate, b_gate, lam, h_bwd, gy]
        out_specs = big
        out_shape = wide
        scratch = scan_scratch
    return pl.pallas_call(
        functools.partial(_lru_kernel, reverse=reverse, n_chunks=n_chunks),
        grid=(batch, n_chunks + 1),
        in_specs=in_specs,
        out_specs=out_specs,
        out_shape=out_shape,
        scratch_shapes=scratch,
        compiler_params=_cparams(("parallel", "arbitrary"), 40),
        name="lru_bwd" if reverse else "lru_fwd",
    )(*args)


def _merge_kernel(oa_ref, ob_ref, sga_ref, sgb_ref, x_ref, mod_ref, woa_ref, wol_ref, wout_ref,
                  n2_ref, wq_ref, keys_ref, x1_ref, h2_ref, st_ref):
    ma = jnp.dot(oa_ref[...], woa_ref[...], preferred_element_type=F32)
    mb = jnp.dot(ob_ref[...].astype(BF16), wol_ref[...], preferred_element_type=F32)
    merged = sga_ref[...] * ma + sgb_ref[...] * mb
    gate1 = mod_ref[:, 2 * D_MODEL:3 * D_MODEL]
    x1 = x_ref[...] + gate1 * jnp.dot(merged.astype(BF16), wout_ref[...], preferred_element_type=F32)
    x1_ref[...] = x1
    ms = jnp.mean(x1 * x1, axis=-1, keepdims=True)
    h = x1 * lax.rsqrt(ms + EPS) * n2_ref[...]
    shift2 = mod_ref[:, 3 * D_MODEL:4 * D_MODEL]
    scale2 = mod_ref[:, 4 * D_MODEL:5 * D_MODEL]
    h2 = (h * (1.0 + scale2) + shift2).astype(BF16)
    h2_ref[...] = h2
    qp = jnp.dot(h2, wq_ref[...], preferred_element_type=F32).astype(BF16)
    for hp in range(2 * PEER_HEADS):
        st_ref[hp] = lax.dot_general(keys_ref[hp], qp[:, hp * PEER_HALF:(hp + 1) * PEER_HALF], _NT,
                                     preferred_element_type=F32)


def _merge(o_a, o_b, sga, sgb, x2, mod3, w_oa, w_ol, w_out, n2, w_q, keys, rows_per_batch):
    t = x2.shape[0]
    tm = 512
    per = rows_per_batch // tm
    row = lambda i: (i, 0)
    const = lambda i: (0, 0)
    wide = pl.BlockSpec((tm, D_MODEL), row)
    nq = PEER_HEADS * PEER_QDIM
    return pl.pallas_call(
        _merge_kernel,
        grid=(t // tm,),
        in_specs=[pl.BlockSpec((tm, NA_WIDTH), row), wide, wide, wide, wide,
                  pl.BlockSpec((None, 1, 6 * D_MODEL), lambda i: (i // per, 0, 0)),
                  _resident((NA_WIDTH, D_MODEL), const),
                  _resident((LRU_WIDTH, D_MODEL), const),
                  _resident((D_MODEL, D_MODEL), const),
                  _resident((1, D_MODEL), const),
                  _resident((D_MODEL, nq), const),
                  _resident((2 * PEER_HEADS, PEER_NKEYS, PEER_HALF), lambda i: (0, 0, 0))],
        out_specs=[wide, wide, pl.BlockSpec((2 * PEER_HEADS, PEER_NKEYS, tm), lambda i: (0, 0, i))],
        out_shape=[jax.ShapeDtypeStruct((t, D_MODEL), F32),
                   jax.ShapeDtypeStruct((t, D_MODEL), BF16),
                   jax.ShapeDtypeStruct((2 * PEER_HEADS, PEER_NKEYS, t), F32)],
        compiler_params=_cparams(("parallel",), 56),
        name="merge",
    )(o_a, o_b, sga, sgb, x2, mod3, w_oa, w_ol, w_out, n2, w_q, keys)


def _top_ranked(x, k, with_rank=True):
    rank = jnp.full(x.shape, float(k), F32) if with_rank else None
    vals = []
    for b in range(k):
        m = jnp.max(x, axis=0, keepdims=True)
        hit = x >= m
        if with_rank:
            rank = jnp.where(hit, float(b), rank)
        x = jnp.where(hit, NEG_INF, x)
        vals.append(m)
    return vals, rank


def _stack_rows(rows):
    cols = rows[0].shape[1]
    slot = lax.broadcasted_iota(jnp.int32, (len(rows), cols), 0)
    out = jnp.zeros((len(rows), cols), F32)
    for b, row in enumerate(rows):
        out = jnp.where(slot == b, row, out)
    return out


_PAIR_LIMIT = [PEER_TOPK // (a + 1) for a in range(PEER_TOPK)]
_PAIR_HALF = PEER_TOPK // 2


def _select_kernel(st_ref, c1_ref, p1_ref, r2_ref, p2_ref):
    k = PEER_TOPK
    cols = st_ref.shape[2]
    slot = lax.broadcasted_iota(jnp.int32, (_PAIR_HALF, cols), 0)
    for h in range(PEER_HEADS):
        s1 = st_ref[2 * h]
        s2 = st_ref[2 * h + 1]
        v1, _ = _top_ranked(s1, k, with_rank=False)
        v2, rank2 = _top_ranked(s2, k)
        v2_all = _stack_rows(v2)
        groups = [v1[0] + v2_all]
        for a in range(1, _PAIR_HALF):
            groups.append(jnp.where(slot < _PAIR_LIMIT[a], v1[a] + v2_all[0:_PAIR_HALF], NEG_INF))
        groups.append(_stack_rows(v1[_PAIR_HALF:]) + v2[0])
        cand0 = jnp.concatenate(groups, axis=0)
        cand = cand0
        tau = jnp.full_like(v1[0], NEG_INF)
        seen = jnp.zeros_like(v1[0])
        for _ in range(k):
            m = jnp.max(cand, axis=0, keepdims=True)
            hit = cand >= m
            tau = jnp.where(seen < k, m, tau)
            seen = seen + jnp.sum(hit.astype(F32), axis=0, keepdims=True)
            cand = jnp.where(hit, NEG_INF, cand)
        picked = cand0 >= tau
        z = jnp.sum(jnp.where(picked, jnp.exp(cand0 - (v1[0] + v2[0])), 0.0), axis=0, keepdims=True)
        pf = picked.astype(F32)
        counts = [jnp.sum(pf[0:k], axis=0, keepdims=True)]
        for a in range(1, _PAIR_HALF):
            lo = k + (a - 1) * _PAIR_HALF
            counts.append(jnp.sum(pf[lo:lo + _PAIR_HALF], axis=0, keepdims=True))
        lo = k + (_PAIR_HALF - 1) * _PAIR_HALF
        counts += [pf[lo + a:lo + a + 1] for a in range(k - _PAIR_HALF)]
        c1 = jnp.zeros_like(s1)
        for a in range(k):
            c1 = jnp.where(s1 == v1[a], counts[a], c1)
        c1_ref[h] = c1
        p1_ref[h] = jnp.exp(s1 - v1[0])
        r2_ref[h * PEER_NKEYS:(h + 1) * PEER_NKEYS, :] = rank2.astype(BF16)
        p2_ref[h * PEER_NKEYS:(h + 1) * PEER_NKEYS, :] = (jnp.exp(s2 - v2[0]) / z).astype(BF16)


def _select(st):
    t = st.shape[2]
    tt = 256
    blk = pl.BlockSpec((PEER_HEADS, PEER_NKEYS, tt), lambda i: (0, 0, i))
    flat = pl.BlockSpec((PEER_HEADS * PEER_NKEYS, tt), lambda i: (0, i))
    full = jax.ShapeDtypeStruct((PEER_HEADS, PEER_NKEYS, t), F32)
    full_flat = jax.ShapeDtypeStruct((PEER_HEADS * PEER_NKEYS, t), BF16)
    return pl.pallas_call(
        _select_kernel,
        grid=(t // tt,),
        in_specs=[pl.BlockSpec((2 * PEER_HEADS, PEER_NKEYS, tt), lambda i: (0, 0, i))],
        out_specs=[blk, blk, flat, flat],
        out_shape=[full, full, full_flat, full_flat],
        compiler_params=_cparams(("parallel",), 40),
        name="select",
    )(st)


EXP_TOK = 512
EXP_BLK = 1024
EXP_I = EXP_BLK // PEER_NKEYS
GATE_ROWS = V7X_SUBLANES
GATE_LANES = 2 * V7X_LANES


def _gate_times_act(at_ref, m_ref, c1_ref, p1_ref, r2_ref, p2_ref):
    jt, jl = GATE_ROWS, GATE_LANES
    for lt in range(EXP_TOK // jl):
        lanes = slice(lt * jl, (lt + 1) * jl)
        c1_rows = [c1_ref[h, :, lanes] for h in range(PEER_HEADS)]
        p1_rows = [p1_ref[h, :, lanes] for h in range(PEER_HEADS)]
        for il in range(EXP_I):
            w = [None] * (PEER_NKEYS // jt)
            for h in range(PEER_HEADS):
                c1 = jnp.broadcast_to(c1_rows[h][il:il + 1, :], (jt, jl)).astype(BF16)
                p1 = jnp.broadcast_to(p1_rows[h][il:il + 1, :], (jt, jl)).astype(BF16)
                for jb in range(PEER_NKEYS // jt):
                    js = slice(h * PEER_NKEYS + jb * jt, h * PEER_NKEYS + (jb + 1) * jt)
                    term = jnp.where(r2_ref[js, lanes] < c1, p2_ref[js, lanes] * p1, 0)
                    w[jb] = term if h == 0 else w[jb] + term
            row0 = il * PEER_NKEYS
            m_ref[row0:row0 + PEER_NKEYS, lanes] = _gelu_bf16(at_ref[row0:row0 + PEER_NKEYS, lanes])
            for jb in range(PEER_NKEYS // jt):
                ms = slice(row0 + jb * jt, row0 + (jb + 1) * jt)
                m_ref[ms, lanes] = w[jb] * m_ref[ms, lanes]


def _experts_kernel(h2_ref, u_ref, vt_ref, c1_ref, p1_ref, r2_ref, p2_ref, x1_ref, mod_ref,
                    o_ref, at_ref, m_ref, yt_ref):
    eb = pl.program_id(1)

    @pl.when(eb == 0)
    def _():
        yt_ref[...] = jnp.zeros_like(yt_ref)

    at_ref[...] = lax.dot_general(u_ref[...], h2_ref[...], _NT, preferred_element_type=F32)
    _gate_times_act(at_ref, m_ref, c1_ref, p1_ref, r2_ref, p2_ref)
    yt_ref[...] += jnp.dot(vt_ref[...], m_ref[...], preferred_element_type=F32)

    @pl.when(eb == pl.num_programs(1) - 1)
    def _():
        gate2 = mod_ref[:, 5 * D_MODEL:6 * D_MODEL]
        o_ref[...] = x1_ref[...] + gate2 * yt_ref[...].T


def _experts(h2, u_b, vt_b, c1, p1, r2, p2, x1, mod3, rows_per_batch):
    t = h2.shape[0]
    per = rows_per_batch // EXP_TOK
    tok = lambda i, e: (i, 0)
    rows = pl.BlockSpec((PEER_HEADS, EXP_I, EXP_TOK), lambda i, e: (0, e, i))
    sel_flat = pl.BlockSpec((PEER_HEADS * PEER_NKEYS, EXP_TOK), lambda i, e: (0, i))
    return pl.pallas_call(
        _experts_kernel,
        grid=(t // EXP_TOK, PEER_EXPERTS // EXP_BLK),
        in_specs=[pl.BlockSpec((EXP_TOK, D_MODEL), tok),
                  pl.BlockSpec((EXP_BLK, D_MODEL), lambda i, e: (e, 0)),
                  pl.BlockSpec((D_MODEL, EXP_BLK), lambda i, e: (0, e)),
                  rows, rows, sel_flat, sel_flat,
                  pl.BlockSpec((EXP_TOK, D_MODEL), tok),
                  pl.BlockSpec((None, 1, 6 * D_MODEL), lambda i, e: (i // per, 0, 0))],
        out_specs=pl.BlockSpec((EXP_TOK, D_MODEL), tok),
        out_shape=jax.ShapeDtypeStruct((t, D_MODEL), F32),
        scratch_shapes=[pltpu.VMEM((EXP_BLK, EXP_TOK), F32),
                        pltpu.VMEM((EXP_BLK, EXP_TOK), BF16),
                        pltpu.VMEM((D_MODEL, EXP_TOK), F32)],
        compiler_params=_cparams(("parallel", "arbitrary"), 48),
        name="experts",
    )(h2, u_b, vt_b, c1, p1, r2, p2, x1, mod3)


def _layer(x, c, ctx, c_ctx, w_ada, b_ada, norm1_w, norm2_w, w_in, q_norm_w, k_norm_w, na_rpb,
           conv_w, conv_b, lru_w_a, lru_b_a, lru_w_x, lru_b_x, lru_lambda, w_o_attn, w_o_lru,
           w_out, peer_w_q, peer_keys, peer_u, peer_v):
    batch, seq, d = x.shape
    n_ctx = ctx.shape[1]
    assert d == D_MODEL and n_ctx == CTX_LEN and seq % (ATT_QROWS * GRID_W) == 0 and batch < MOD_ROWS
    x2 = x.reshape(batch * seq, d)
    ctx2 = ctx.reshape(batch * n_ctx, d)

    c_rows = jnp.zeros((MOD_ROWS, d), F32).at[:batch].set(c).at[batch].set(c_ctx)
    mod3 = _mod(c_rows, w_ada, b_ada).reshape(MOD_ROWS, 1, 6 * d)

    w_in_b = w_in.astype(BF16)
    n1 = norm1_w.reshape(1, d)
    qn_row = (jnp.tile(q_norm_w, NA_HEADS) * (NA_HEAD_DIM ** -0.5)).reshape(1, NA_WIDTH)
    kn_row = jnp.tile(k_norm_w, NA_HEADS).reshape(1, NA_WIDTH)
    qkv, xr, gy, sga, sgb = _inproj(x2, mod3, n1, w_in_b, qn_row, kn_row, seq)
    kv_c, xr_c = _inproj_ctx(ctx2, mod3, n1, w_in_b, kn_row, batch)

    o_a = _attention(qkv, kv_c, _attn_bias_table(na_rpb), batch, seq)

    def gate_params(direction):
        w_gate = jnp.concatenate([lru_w_a[direction], lru_w_x[direction]], axis=-1).astype(BF16)
        b_gate = jnp.stack([lru_b_a[direction], lru_b_x[direction]])
        return w_gate, b_gate, lru_lambda[direction].reshape(1, LRU_WIDTH)

    h_bwd, u, u_c = _lru_pass(xr, xr_c, *gate_params(1), batch, seq, True,
                              conv=(conv_w, conv_b.reshape(1, LRU_WIDTH)))
    o_b = _lru_pass(u, u_c, *gate_params(0), batch, seq, False, h_bwd=h_bwd, gy=gy)

    keys = peer_keys.reshape(2 * PEER_HEADS, PEER_NKEYS, PEER_HALF).astype(BF16)
    x1, h2, st = _merge(o_a, o_b, sga, sgb, x2, mod3, w_o_attn.astype(BF16), w_o_lru.astype(BF16),
                        w_out.astype(BF16), norm2_w.reshape(1, d), peer_w_q.astype(BF16), keys, seq)
    c1, p1, r2, p2 = _select(st)
    out = _experts(h2, peer_u, peer_v.astype(BF16).T, c1, p1, r2, p2, x1, mod3, seq)
    return out.reshape(batch, seq, d)


def kernel(x, c, ctx, c_ctx, w_ada, b_ada, norm1_w, norm2_w, w_in, q_norm_w, k_norm_w, na_rpb, conv_w, conv_b, lru_w_a, lru_b_a, lru_w_x, lru_b_x, lru_lambda, w_o_attn, w_o_lru, w_out, peer_w_q, peer_keys, peer_u, peer_v):
    depth = w_ada.shape[0]
    for layer in range(depth):
        x = _layer(x, c, ctx, c_ctx, w_ada[layer], b_ada[layer], norm1_w[layer], norm2_w[layer],
                   w_in[layer], q_norm_w[layer], k_norm_w[layer], na_rpb[layer], conv_w[layer],
                   conv_b[layer], lru_w_a[layer], lru_b_a[layer], lru_w_x[layer], lru_b_x[layer],
                   lru_lambda[layer], w_o_attn[layer], w_o_lru[layer], w_out[layer], peer_w_q[layer],
                   peer_keys[layer], peer_u[layer], peer_v[layer])
    return x
```

```python
import functools

import numpy as np
import jax
import jax.numpy as jnp
from jax import lax
from jax.experimental import pallas as pl
from jax.experimental.pallas import tpu as pltpu

F32 = jnp.float32
BF16 = jnp.bfloat16

D_MODEL = 1024
GRID_W = 64
CTX_LEN = 256
NA_HEADS = 8
NA_HEAD_DIM = 64
NA_WIDTH = NA_HEADS * NA_HEAD_DIM
NA_WIN_ROWS = 8
NA_WIN_COLS = 16
LRU_WIDTH = D_MODEL
LRU_BLOCKS = 8
LRU_BLOCK = LRU_WIDTH // LRU_BLOCKS
CONV_WIDTH = 4
CONV_LEFT = 2
LRU_C = 8.0
PEER_HEADS = 8
PEER_NKEYS = 128
PEER_EXPERTS = PEER_NKEYS * PEER_NKEYS
PEER_QDIM = 256
PEER_HALF = PEER_QDIM // 2
PEER_TOPK = 16
EPS = 1e-6
NEG_INF = -1e30

Q0 = 0
K0 = Q0 + NA_WIDTH
V0 = K0 + NA_WIDTH
XR0 = V0 + NA_WIDTH
YR0 = XR0 + LRU_WIDTH
GA0 = YR0 + LRU_WIDTH
GB0 = GA0 + D_MODEL
IN_COLS = GB0 + D_MODEL

V7X_LANES = 128
V7X_SUBLANES = 8
BF16_SUBLANES = 2 * V7X_SUBLANES
V7X_VMEM_BYTES = 64 * 1024 * 1024

MOD_ROWS = 8
ATT_QROWS = 8
ATT_SUB = 4
ATT_KROWS = 12
LRU_CHUNK = 256
HALO = V7X_SUBLANES

_NT = (((1,), (1,)), ((), ()))


def _cparams(sem, vmem_mb, flags=None):
    return pltpu.CompilerParams(dimension_semantics=sem, vmem_limit_bytes=vmem_mb * 1024 * 1024, flags=flags)


def _resident(shape, index_map):
    return pl.BlockSpec(shape, index_map, pipeline_mode=pl.Buffered(1))


def _gelu(x):
    return jax.nn.gelu(x)


def _sigmoid(x):
    return jax.nn.sigmoid(x)


_GELU_A = -2.0 * float(np.sqrt(2.0 / np.pi) * np.log2(np.e))
_GELU_B = _GELU_A * 0.044715


def _gelu_bf16(x):
    xb = x.astype(BF16)
    z = (xb * xb * _GELU_B + _GELU_A) * xb
    return xb / (1.0 + jnp.exp2(z))


def _mod_kernel(c_ref, w_ref, b_ref, o_ref):
    cs = c_ref[...]
    s = cs * _sigmoid(cs)
    o_ref[...] = jnp.dot(s, w_ref[...], preferred_element_type=F32,
                         precision=lax.Precision.HIGHEST) + b_ref[...]


def _mod(c_rows, w_ada, b_ada):
    n = w_ada.shape[1]
    tn = 1536
    return pl.pallas_call(
        _mod_kernel,
        grid=(n // tn,),
        in_specs=[pl.BlockSpec((MOD_ROWS, D_MODEL), lambda j: (0, 0)),
                  pl.BlockSpec((D_MODEL, tn), lambda j: (0, j)),
                  pl.BlockSpec((1, tn), lambda j: (0, j))],
        out_specs=pl.BlockSpec((MOD_ROWS, tn), lambda j: (0, j)),
        out_shape=jax.ShapeDtypeStruct((MOD_ROWS, n), F32),
        compiler_params=_cparams(("arbitrary",), 40),
        name="mod",
    )(c_rows, w_ada, b_ada.reshape(1, n))


def _normed_input(x_ref, mod_ref, n1_ref):
    x = x_ref[...]
    ms = jnp.mean(x * x, axis=-1, keepdims=True)
    h = x * lax.rsqrt(ms + EPS) * n1_ref[...]
    shift = mod_ref[:, 0:D_MODEL]
    scale = mod_ref[:, D_MODEL:2 * D_MODEL]
    return (h * (1.0 + scale) + shift).astype(BF16)


def _head_norm(z, g_ref, w_row):
    zz = z * z
    hi = zz.astype(BF16)
    lo = (zz - hi.astype(F32)).astype(BF16)
    ms = (jnp.dot(hi, g_ref[...], preferred_element_type=F32)
          + jnp.dot(lo, g_ref[...], preferred_element_type=F32))
    return z * lax.rsqrt(ms + EPS) * w_row


def _inproj_kernel(x_ref, mod_ref, n1_ref, w_ref, g_ref, qn_ref, kn_ref,
                   qkv_ref, xr_ref, gy_ref, sga_ref, sgb_ref):
    hb = _normed_input(x_ref, mod_ref, n1_ref)
    seg = lambda lo, hi: jnp.dot(hb, w_ref[:, lo:hi], preferred_element_type=F32)
    qkv_ref[:, Q0:K0] = _head_norm(seg(Q0, K0), g_ref, qn_ref[...]).astype(BF16)
    qkv_ref[:, K0:V0] = _head_norm(seg(K0, V0), g_ref, kn_ref[...]).astype(BF16)
    qkv_ref[:, V0:XR0] = seg(V0, XR0).astype(BF16)
    xr_ref[...] = seg(XR0, YR0)
    gy_ref[...] = _gelu(seg(YR0, GA0))
    sga_ref[...] = _sigmoid(seg(GA0, GB0))
    sgb_ref[...] = _sigmoid(seg(GB0, IN_COLS))


def _inproj_ctx_kernel(x_ref, mod_ref, n1_ref, w_ref, g_ref, kn_ref, kv_ref, xr_ref):
    hb = _normed_input(x_ref, mod_ref, n1_ref)
    seg = lambda lo, hi: jnp.dot(hb, w_ref[:, lo:hi], preferred_element_type=F32)
    kv_ref[:, 0:NA_WIDTH] = _head_norm(seg(K0, V0), g_ref, kn_ref[...]).astype(BF16)
    kv_ref[:, NA_WIDTH:2 * NA_WIDTH] = seg(V0, XR0).astype(BF16)
    xr_ref[...] = seg(XR0, YR0)


def _head_mean_matrix():
    head = np.arange(NA_WIDTH) // NA_HEAD_DIM
    return jnp.asarray((head[:, None] == head[None, :]).astype(np.float32) / NA_HEAD_DIM, dtype=BF16)


def _inproj(x2, mod3, n1, w_in_b, qn_row, kn_row, rows_per_batch):
    t = x2.shape[0]
    tm = 512
    per = rows_per_batch // tm
    row = lambda i: (i, 0)
    const = lambda i: (0, 0)
    wide = pl.BlockSpec((tm, D_MODEL), row)
    return pl.pallas_call(
        _inproj_kernel,
        grid=(t // tm,),
        in_specs=[wide,
                  pl.BlockSpec((None, 1, 6 * D_MODEL), lambda i: (i // per, 0, 0)),
                  _resident((1, D_MODEL), const),
                  _resident((D_MODEL, IN_COLS), const),
                  _resident((NA_WIDTH, NA_WIDTH), const),
                  _resident((1, NA_WIDTH), const),
                  _resident((1, NA_WIDTH), const)],
        out_specs=[pl.BlockSpec((tm, 3 * NA_WIDTH), row), wide, wide, wide, wide],
        out_shape=[jax.ShapeDtypeStruct((t, 3 * NA_WIDTH), BF16)] + [jax.ShapeDtypeStruct((t, D_MODEL), F32)] * 4,
        compiler_params=_cparams(("parallel",), 56),
        name="inproj",
    )(x2, mod3, n1, w_in_b, _head_mean_matrix(), qn_row, kn_row)


def _inproj_ctx(ctx2, mod3, n1, w_in_b, kn_row, ctx_mod_row):
    t = ctx2.shape[0]
    tm = 512
    row = lambda i: (i, 0)
    const = lambda i: (0, 0)
    wide = pl.BlockSpec((tm, D_MODEL), row)
    return pl.pallas_call(
        _inproj_ctx_kernel,
        grid=(t // tm,),
        in_specs=[wide,
                  pl.BlockSpec((None, 1, 6 * D_MODEL), lambda i: (ctx_mod_row, 0, 0)),
                  _resident((1, D_MODEL), const),
                  _resident((D_MODEL, IN_COLS), const),
                  _resident((NA_WIDTH, NA_WIDTH), const),
                  _resident((1, NA_WIDTH), const)],
        out_specs=[wide, wide],
        out_shape=[jax.ShapeDtypeStruct((t, 2 * NA_WIDTH), BF16), jax.ShapeDtypeStruct((t, D_MODEL), F32)],
        compiler_params=_cparams(("parallel",), 56),
        name="inproj_ctx",
    )(ctx2, mod3, n1, w_in_b, _head_mean_matrix(), kn_row)


def _attn_bias_table(rpb):
    wr, wc, w = NA_WIN_ROWS, NA_WIN_COLS, GRID_W
    col = np.arange(w)
    col0 = np.clip(col - wc // 2, 0, w - wc)
    col_mask = (col[None, :] >= col0[:, None]) & (col[None, :] < col0[:, None] + wc)
    dc = np.clip(col[None, :] - col[:, None], -(wc - 1), wc - 1) + wc - 1
    onehot = (dc[None] == np.arange(2 * wc - 1)[:, None, None]).astype(np.float32)
    tab = jnp.einsum('hdc,cqk->hdqk', rpb, onehot, precision=lax.Precision.HIGHEST)
    tab = jnp.where(col_mask[None, None], tab, NEG_INF)
    none = jnp.full((NA_HEADS, 1, w, w), NEG_INF, F32)
    return jnp.concatenate([jnp.concatenate([none, tab], axis=1),
                            jnp.concatenate([tab, none], axis=1)], axis=-1)


def _att_key_start(first_qrow, rows):
    return np.clip(first_qrow - NA_WIN_ROWS // 2, 0, rows - ATT_KROWS)


def _fill_bias(tab_ref, bias_ref, rb, rows):
    wr, w = NA_WIN_ROWS, GRID_W
    lane = lax.broadcasted_iota(jnp.int32, (w, 2 * w), 1)
    for hh in range(V7X_LANES // NA_HEAD_DIM):
        for sub in range(ATT_QROWS // ATT_SUB):
            ks = int(_att_key_start(rb * ATT_QROWS + sub * ATT_SUB, rows))
            for i in range(ATT_SUB):
                r = rb * ATT_QROWS + sub * ATT_SUB + i
                row0 = int(np.clip(r - wr // 2, 0, rows - wr))
                assert ks <= row0 and row0 + wr <= ks + ATT_KROWS
                for w2 in range(ATT_KROWS // 2):
                    kr = ks + 2 * w2
                    left = row0 <= kr < row0 + wr
                    right = row0 <= kr + 1 < row0 + wr
                    if left or right:
                        blk = tab_ref[hh, kr - r + wr]
                        if not right:
                            blk = jnp.where(lane < w, blk, NEG_INF)
                        elif not left:
                            blk = jnp.where(lane >= w, blk, NEG_INF)
                    else:
                        blk = jnp.full((w, 2 * w), NEG_INF, F32)
                    bias_ref[hh, sub, i * w:(i + 1) * w, w2 * 2 * w:(w2 + 1) * 2 * w] = blk


def _attn_kernel(q_ref, k_ref, v_ref, kc_ref, vc_ref, tab_ref, o_ref, bias_ref, *, rows):
    rb = pl.program_id(1)
    n_rb = rows // ATT_QROWS

    for rb_static in (0, 1, n_rb - 1):
        @pl.when(jnp.logical_and(pl.program_id(2) == 0, rb == rb_static))
        def _():
            _fill_bias(tab_ref, bias_ref, rb_static, rows)

    nq, nk = ATT_SUB * GRID_W, ATT_KROWS * GRID_W
    kc = kc_ref[...]
    vc = vc_ref[...]
    lane = lax.broadcasted_iota(jnp.int32, (1, V7X_LANES), 1)
    for sub in range(ATT_QROWS // ATT_SUB):
        first = rb * ATT_QROWS + sub * ATT_SUB
        ks = jnp.clip(first - NA_WIN_ROWS // 2, 0, rows - ATT_KROWS) * GRID_W
        ks = pl.multiple_of(ks, GRID_W)
        q = q_ref[sub * nq:(sub + 1) * nq, :]
        kw = k_ref[pl.ds(ks, nk), :]
        vw = v_ref[pl.ds(ks, nk), :]
        acc = jnp.zeros(q.shape, F32)
        for hh in range(V7X_LANES // NA_HEAD_DIM):
            sel = (lane // NA_HEAD_DIM == hh).astype(BF16)
            qh = q * sel
            s = lax.dot_general(qh, kw, _NT, preferred_element_type=F32) + bias_ref[hh, sub]
            sc = lax.dot_general(qh, kc, _NT, preferred_element_type=F32)
            mx = jnp.maximum(jnp.max(s, axis=-1, keepdims=True), jnp.max(sc, axis=-1, keepdims=True))
            p = jnp.exp(s - mx)
            pc = jnp.exp(sc - mx)
            den = jnp.sum(p, axis=-1, keepdims=True) + jnp.sum(pc, axis=-1, keepdims=True)
            o = (jnp.dot(p.astype(BF16), vw * sel, preferred_element_type=F32)
                 + jnp.dot(pc.astype(BF16), vc * sel, preferred_element_type=F32))
            acc = acc + o / den
        o_ref[sub * nq:(sub + 1) * nq, :] = acc.astype(BF16)


def _attention(qkv, kv_c, tab, batch, seq):
    rows = seq // GRID_W
    n_rb = rows // ATT_QROWS
    tq = ATT_QROWS * GRID_W
    pairs = NA_WIDTH // V7X_LANES
    kcol, vcol = K0 // V7X_LANES, V0 // V7X_LANES
    assert n_rb >= 3
    heads_per_step = V7X_LANES // NA_HEAD_DIM
    return pl.pallas_call(
        functools.partial(_attn_kernel, rows=rows),
        grid=(pairs, n_rb, batch),
        in_specs=[pl.BlockSpec((tq, V7X_LANES), lambda hp, rb, b: (b * n_rb + rb, hp)),
                  pl.BlockSpec((seq, V7X_LANES), lambda hp, rb, b: (b, kcol + hp)),
                  pl.BlockSpec((seq, V7X_LANES), lambda hp, rb, b: (b, vcol + hp)),
                  pl.BlockSpec((CTX_LEN, V7X_LANES), lambda hp, rb, b: (b, hp)),
                  pl.BlockSpec((CTX_LEN, V7X_LANES), lambda hp, rb, b: (b, pairs + hp)),
                  pl.BlockSpec((heads_per_step, 2 * NA_WIN_ROWS, GRID_W, 2 * GRID_W),
                               lambda hp, rb, b: (hp, 0, 0, 0))],
        out_specs=pl.BlockSpec((tq, V7X_LANES), lambda hp, rb, b: (b * n_rb + rb, hp)),
        out_shape=jax.ShapeDtypeStruct((batch * seq, NA_WIDTH), BF16),
        scratch_shapes=[pltpu.VMEM((heads_per_step, ATT_QROWS // ATT_SUB, ATT_SUB * GRID_W, ATT_KROWS * GRID_W),
                                   F32)],
        compiler_params=_cparams(("arbitrary", "arbitrary", "arbitrary"), 48),
        name="attn",
    )(qkv, qkv, qkv, kv_c, kv_c, tab)


def _lru_kernel(*refs, reverse, n_chunks):
    if reverse:
        (x_ref, xc_ref, prev_ref, next_ref, cw_ref, cb_ref, wg_ref, bg_ref, lam_ref,
         h_ref, u_ref, uc_ref, win_ref, a_ref, b_ref, carry_ref) = refs
    else:
        (u_ref, uc_ref, wg_ref, bg_ref, lam_ref, hb_ref, gy_ref, h_ref, a_ref, b_ref, carry_ref) = refs
    c = pl.program_id(1)
    is_ctx = c == 0
    n = LRU_CHUNK

    if reverse:
        chunk = n_chunks - c
        has_prev = jnp.logical_and(c > 0, chunk > 0)
        has_next = jnp.logical_and(c > 0, chunk < n_chunks - 1)
        win_ref[0:HALO, :] = jnp.where(has_prev, prev_ref[...], 0.0)
        win_ref[HALO:HALO + n, :] = jnp.where(is_ctx, xc_ref[...], x_ref[...])
        win_ref[HALO + n:2 * HALO + n, :] = jnp.where(has_next, next_ref[...], 0.0)
        u = cb_ref[...]
        for j in range(CONV_WIDTH):
            off = HALO - CONV_LEFT + j
            u = u + cw_ref[j:j + 1, :] * win_ref[off:off + n, :]
        u_ref[...] = u

        @pl.when(is_ctx)
        def _():
            uc_ref[...] = u
    else:
        u = jnp.where(is_ctx, uc_ref[...], u_ref[...])

    ub = u.astype(BF16)
    lam = lam_ref[...]
    softplus = jnp.maximum(-lam, 0.0) + jnp.log(1.0 + jnp.exp(-jnp.abs(lam)))
    for blk in range(LRU_BLOCKS):
        lo, hi = blk * LRU_BLOCK, (blk + 1) * LRU_BLOCK
        g = jnp.dot(ub[:, lo:hi], wg_ref[blk], preferred_element_type=F32)
        r = _sigmoid(g[:, 0:LRU_BLOCK] + bg_ref[0:1, lo:hi])
        i = _sigmoid(g[:, LRU_BLOCK:2 * LRU_BLOCK] + bg_ref[1:2, lo:hi])
        log_a = -LRU_C * r * softplus[:, lo:hi]
        a = jnp.exp(log_a)
        beta = jnp.sqrt(1.0 - a * a)
        a_ref[:, lo:hi] = a
        b_ref[:, lo:hi] = beta * (i * u[:, lo:hi])

    groups = n // V7X_SUBLANES
    a = a_ref[...].reshape(groups, V7X_SUBLANES, LRU_WIDTH)
    b = b_ref[...].reshape(groups, V7X_SUBLANES, LRU_WIDTH)
    pos = lax.broadcasted_iota(jnp.int32, (1, V7X_SUBLANES, 1), 1)
    for s in (1, 2, 4):
        shift = (V7X_SUBLANES - s) if reverse else s
        a_n = pltpu.roll(a, shift, axis=1)
        b_n = pltpu.roll(b, shift, axis=1)
        ok = (pos < V7X_SUBLANES - s) if reverse else (pos >= s)
        b = jnp.where(ok, a * b_n + b, b)
        a = jnp.where(ok, a * a_n, a)
    a_ref[...] = a.reshape(n, LRU_WIDTH)
    b_ref[...] = b.reshape(n, LRU_WIDTH)

    @pl.when(is_ctx)
    def _():
        carry_ref[...] = jnp.zeros_like(carry_ref)

    edge = 0 if reverse else V7X_SUBLANES - 1

    def group_step(gi, carry):
        g = (groups - 1 - gi) if reverse else gi
        row = pl.multiple_of(g * V7X_SUBLANES, V7X_SUBLANES)
        hg = a_ref[pl.ds(row, V7X_SUBLANES), :] * carry + b_ref[pl.ds(row, V7X_SUBLANES), :]
        if reverse:
            h_ref[pl.ds(row, V7X_SUBLANES), :] = hg
        else:
            h_ref[pl.ds(row, V7X_SUBLANES), :] = ((hg + hb_ref[pl.ds(row, V7X_SUBLANES), :])
                                                  * gy_ref[pl.ds(row, V7X_SUBLANES), :])
        return hg[edge:edge + 1, :]

    carry_ref[...] = lax.fori_loop(0, groups, group_step, carry_ref[...])


def _lru_pass(x_in, xc_in, w_gate, b_gate, lam, batch, seq, reverse, conv=None, h_bwd=None, gy=None):
    n = LRU_CHUNK
    n_chunks = seq // n
    per_halo = n // HALO

    def chunk_of(c):
        return jnp.where(c == 0, (n_chunks - 1) if reverse else 0, (n_chunks - c) if reverse else (c - 1))

    cur = lambda b, c: (b * n_chunks + chunk_of(c), 0)
    prev = lambda b, c: (jnp.maximum((b * n_chunks + chunk_of(c)) * per_halo - 1, 0), 0)
    nxt = lambda b, c: (jnp.minimum((b * n_chunks + chunk_of(c) + 1) * per_halo, batch * seq // HALO - 1), 0)
    const2 = lambda b, c: (0, 0)
    const3 = lambda b, c: (0, 0, 0)
    big = pl.BlockSpec((n, LRU_WIDTH), cur)
    ctx_blk = pl.BlockSpec((n, LRU_WIDTH), lambda b, c: (b, 0))
    gates = [_resident((LRU_BLOCKS, LRU_BLOCK, 2 * LRU_BLOCK), const3),
             _resident((2, LRU_WIDTH), const2),
             _resident((1, LRU_WIDTH), const2)]
    wide = jax.ShapeDtypeStruct((batch * seq, LRU_WIDTH), F32)
    scan_scratch = [pltpu.VMEM((n, LRU_WIDTH), F32), pltpu.VMEM((n, LRU_WIDTH), F32), pltpu.VMEM((1, LRU_WIDTH), F32)]
    if reverse:
        conv_w, conv_b = conv
        in_specs = [big, ctx_blk, pl.BlockSpec((HALO, LRU_WIDTH), prev), pl.BlockSpec((HALO, LRU_WIDTH), nxt),
                    _resident((CONV_WIDTH, LRU_WIDTH), const2), _resident((1, LRU_WIDTH), const2)] + gates
        args = [x_in, xc_in, x_in, x_in, conv_w, conv_b, w_gate, b_gate, lam]
        out_specs = [big, big, ctx_blk]
        out_shape = [wide, wide, jax.ShapeDtypeStruct(xc_in.shape, F32)]
        scratch = [pltpu.VMEM((n + 2 * HALO, LRU_WIDTH), F32)] + scan_scratch
    else:
        in_specs = [big, ctx_blk] + gates + [big, big]
        args = [x_in, xc_in, w_gate, b_gate, lam, h_bwd, gy]
        out_specs = big
        out_shape = wide
        scratch = scan_scratch
    return pl.pallas_call(
        functools.partial(_lru_kernel, reverse=reverse, n_chunks=n_chunks),
        grid=(batch, n_chunks + 1),
        in_specs=in_specs,
        out_specs=out_specs,
        out_shape=out_shape,
        scratch_shapes=scratch,
        compiler_params=_cparams(("parallel", "arbitrary"), 40),
        name="lru_bwd" if reverse else "lru_fwd",
    )(*args)


def _merge_kernel(oa_ref, ob_ref, sga_ref, sgb_ref, x_ref, mod_ref, woa_ref, wol_ref, wout_ref,
                  n2_ref, wq_ref, keys_ref, x1_ref, h2_ref, st_ref):
    ma = jnp.dot(oa_ref[...], woa_ref[...], preferred_element_type=F32)
    mb = jnp.dot(ob_ref[...].astype(BF16), wol_ref[...], preferred_element_type=F32)
    merged = sga_ref[...] * ma + sgb_ref[...] * mb
    gate1 = mod_ref[:, 2 * D_MODEL:3 * D_MODEL]
    x1 = x_ref[...] + gate1 * jnp.dot(merged.astype(BF16), wout_ref[...], preferred_element_type=F32)
    x1_ref[...] = x1
    ms = jnp.mean(x1 * x1, axis=-1, keepdims=True)
    h = x1 * lax.rsqrt(ms + EPS) * n2_ref[...]
    shift2 = mod_ref[:, 3 * D_MODEL:4 * D_MODEL]
    scale2 = mod_ref[:, 4 * D_MODEL:5 * D_MODEL]
    h2 = (h * (1.0 + scale2) + shift2).astype(BF16)
    h2_ref[...] = h2
    qp = jnp.dot(h2, wq_ref[...], preferred_element_type=F32).astype(BF16)
    for hp in range(2 * PEER_HEADS):
        st_ref[hp] = lax.dot_general(keys_ref[hp], qp[:, hp * PEER_HALF:(hp + 1) * PEER_HALF], _NT,
                                     preferred_element_type=F32)


def _merge(o_a, o_b, sga, sgb, x2, mod3, w_oa, w_ol, w_out, n2, w_q, keys, rows_per_batch):
    t = x2.shape[0]
    tm = 512
    per = rows_per_batch // tm
    row = lambda i: (i, 0)
    const = lambda i: (0, 0)
    wide = pl.BlockSpec((tm, D_MODEL), row)
    nq = PEER_HEADS * PEER_QDIM
    return pl.pallas_call(
        _merge_kernel,
        grid=(t // tm,),
        in_specs=[pl.BlockSpec((tm, NA_WIDTH), row), wide, wide, wide, wide,
                  pl.BlockSpec((None, 1, 6 * D_MODEL), lambda i: (i // per, 0, 0)),
                  _resident((NA_WIDTH, D_MODEL), const),
                  _resident((LRU_WIDTH, D_MODEL), const),
                  _resident((D_MODEL, D_MODEL), const),
                  _resident((1, D_MODEL), const),
                  _resident((D_MODEL, nq), const),
                  _resident((2 * PEER_HEADS, PEER_NKEYS, PEER_HALF), lambda i: (0, 0, 0))],
        out_specs=[wide, wide, pl.BlockSpec((2 * PEER_HEADS, PEER_NKEYS, tm), lambda i: (0, 0, i))],
        out_shape=[jax.ShapeDtypeStruct((t, D_MODEL), F32),
                   jax.ShapeDtypeStruct((t, D_MODEL), BF16),
                   jax.ShapeDtypeStruct((2 * PEER_HEADS, PEER_NKEYS, t), F32)],
        compiler_params=_cparams(("parallel",), 56),
        name="merge",
    )(o_a, o_b, sga, sgb, x2, mod3, w_oa, w_ol, w_out, n2, w_q, keys)


def _top_ranked(x, k, with_rank=True):
    rank = jnp.full(x.shape, float(k), F32) if with_rank else None
    vals = []
    for b in range(k):
        m = jnp.max(x, axis=0, keepdims=True)
        hit = x >= m
        if with_rank:
            rank = jnp.where(hit, float(b), rank)
        x = jnp.where(hit, NEG_INF, x)
        vals.append(m)
    return vals, rank


def _stack_rows(rows):
    cols = rows[0].shape[1]
    slot = lax.broadcasted_iota(jnp.int32, (len(rows), cols), 0)
    out = jnp.zeros((len(rows), cols), F32)
    for b, row in enumerate(rows):
        out = jnp.where(slot == b, row, out)
    return out


_PAIR_LIMIT = [PEER_TOPK // (a + 1) for a in range(PEER_TOPK)]
_PAIR_HALF = PEER_TOPK // 2


def _select_kernel(st_ref, c1_ref, p1_ref, r2_ref, p2_ref):
    k = PEER_TOPK
    cols = st_ref.shape[2]
    slot = lax.broadcasted_iota(jnp.int32, (_PAIR_HALF, cols), 0)
    for h in range(PEER_HEADS):
        s1 = st_ref[2 * h]
        s2 = st_ref[2 * h + 1]
        v1, _ = _top_ranked(s1, k, with_rank=False)
        v2, rank2 = _top_ranked(s2, k)
        v2_all = _stack_rows(v2)
        groups = [v1[0] + v2_all]
        for a in range(1, _PAIR_HALF):
            groups.append(jnp.where(slot < _PAIR_LIMIT[a], v1[a] + v2_all[0:_PAIR_HALF], NEG_INF))
        groups.append(_stack_rows(v1[_PAIR_HALF:]) + v2[0])
        cand0 = jnp.concatenate(groups, axis=0)
        cand = cand0
        tau = jnp.full_like(v1[0], NEG_INF)
        seen = jnp.zeros_like(v1[0])
        for _ in range(k):
            m = jnp.max(cand, axis=0, keepdims=True)
            hit = cand >= m
            tau = jnp.where(seen < k, m, tau)
            seen = seen + jnp.sum(hit.astype(F32), axis=0, keepdims=True)
            cand = jnp.where(hit, NEG_INF, cand)
        picked = cand0 >= tau
        z = jnp.sum(jnp.where(picked, jnp.exp(cand0 - (v1[0] + v2[0])), 0.0), axis=0, keepdims=True)
        pf = picked.astype(F32)
        counts = [jnp.sum(pf[0:k], axis=0, keepdims=True)]
        for a in range(1, _PAIR_HALF):
            lo = k + (a - 1) * _PAIR_HALF
            counts.append(jnp.sum(pf[lo:lo + _PAIR_HALF], axis=0, keepdims=True))
        lo = k + (_PAIR_HALF - 1) * _PAIR_HALF
        counts += [pf[lo + a:lo + a + 1] for a in range(k - _PAIR_HALF)]
        c1 = jnp.zeros_like(s1)
        for a in range(k):
            c1 = jnp.where(s1 == v1[a], counts[a], c1)
        c1_ref[h] = c1
        p1_ref[h] = jnp.exp(s1 - v1[0])
        r2_ref[h * PEER_NKEYS:(h + 1) * PEER_NKEYS, :] = rank2.astype(BF16)
        p2_ref[h * PEER_NKEYS:(h + 1) * PEER_NKEYS, :] = (jnp.exp(s2 - v2[0]) / z).astype(BF16)


def _select(st):
    t = st.shape[2]
    tt = 256
    blk = pl.BlockSpec((PEER_HEADS, PEER_NKEYS, tt), lambda i: (0, 0, i))
    flat = pl.BlockSpec((PEER_HEADS * PEER_NKEYS, tt), lambda i: (0, i))
    full = jax.ShapeDtypeStruct((PEER_HEADS, PEER_NKEYS, t), F32)
    full_flat = jax.ShapeDtypeStruct((PEER_HEADS * PEER_NKEYS, t), BF16)
    return pl.pallas_call(
        _select_kernel,
        grid=(t // tt,),
        in_specs=[pl.BlockSpec((2 * PEER_HEADS, PEER_NKEYS, tt), lambda i: (0, 0, i))],
        out_specs=[blk, blk, flat, flat],
        out_shape=[full, full, full_flat, full_flat],
        compiler_params=_cparams(("parallel",), 40),
        name="select",
    )(st)


EXP_TOK = 512
EXP_BLK = 2048
EXP_I = EXP_BLK // PEER_NKEYS
GATE_ROWS = V7X_SUBLANES
GATE_LANES = 2 * V7X_LANES


def _gate_times_act(at_ref, m_ref, c1_ref, p1_ref, r2_ref, p2_ref):
    jt, jl = GATE_ROWS, GATE_LANES
    for lt in range(EXP_TOK // jl):
        lanes = slice(lt * jl, (lt + 1) * jl)
        c1_rows = [c1_ref[h, :, lanes] for h in range(PEER_HEADS)]
        p1_rows = [p1_ref[h, :, lanes] for h in range(PEER_HEADS)]
        for il in range(EXP_I):
            w = [None] * (PEER_NKEYS // jt)
            for h in range(PEER_HEADS):
                c1 = jnp.broadcast_to(c1_rows[h][il:il + 1, :], (jt, jl)).astype(BF16)
                p1 = jnp.broadcast_to(p1_rows[h][il:il + 1, :], (jt, jl)).astype(BF16)
                for jb in range(PEER_NKEYS // jt):
                    js = slice(h * PEER_NKEYS + jb * jt, h * PEER_NKEYS + (jb + 1) * jt)
                    term = jnp.where(r2_ref[js, lanes] < c1, p2_ref[js, lanes] * p1, 0)
                    w[jb] = term if h == 0 else w[jb] + term
            row0 = il * PEER_NKEYS
            m_ref[row0:row0 + PEER_NKEYS, lanes] = _gelu_bf16(at_ref[row0:row0 + PEER_NKEYS, lanes])
            for jb in range(PEER_NKEYS // jt):
                ms = slice(row0 + jb * jt, row0 + (jb + 1) * jt)
                m_ref[ms, lanes] = w[jb] * m_ref[ms, lanes]


def _experts_kernel(h2_ref, u_ref, vt_ref, c1_ref, p1_ref, r2_ref, p2_ref, x1_ref, mod_ref,
                    o_ref, at_ref, m_ref, yt_ref):
    eb = pl.program_id(1)

    @pl.when(eb == 0)
    def _():
        yt_ref[...] = jnp.zeros_like(yt_ref)

    at_ref[...] = lax.dot_general(u_ref[...], h2_ref[...], _NT, preferred_element_type=F32)
    _gate_times_act(at_ref, m_ref, c1_ref, p1_ref, r2_ref, p2_ref)
    yt_ref[...] += jnp.dot(vt_ref[...], m_ref[...], preferred_element_type=F32)

    @pl.when(eb == pl.num_programs(1) - 1)
    def _():
        gate2 = mod_ref[:, 5 * D_MODEL:6 * D_MODEL]
        o_ref[...] = x1_ref[...] + gate2 * yt_ref[...].T


def _experts(h2, u_b, vt_b, c1, p1, r2, p2, x1, mod3, rows_per_batch):
    t = h2.shape[0]
    per = rows_per_batch // EXP_TOK
    tok = lambda i, e: (i, 0)
    rows = pl.BlockSpec((PEER_HEADS, EXP_I, EXP_TOK), lambda i, e: (0, e, i))
    sel_flat = pl.BlockSpec((PEER_HEADS * PEER_NKEYS, EXP_TOK), lambda i, e: (0, i))
    return pl.pallas_call(
        _experts_kernel,
        grid=(t // EXP_TOK, PEER_EXPERTS // EXP_BLK),
        in_specs=[pl.BlockSpec((EXP_TOK, D_MODEL), tok),
                  pl.BlockSpec((EXP_BLK, D_MODEL), lambda i, e: (e, 0)),
                  pl.BlockSpec((D_MODEL, EXP_BLK), lambda i, e: (0, e)),
                  rows, rows, sel_flat, sel_flat,
                  pl.BlockSpec((EXP_TOK, D_MODEL), tok),
                  pl.BlockSpec((None, 1, 6 * D_MODEL), lambda i, e: (i // per, 0, 0))],
        out_specs=pl.BlockSpec((EXP_TOK, D_MODEL), tok),
        out_shape=jax.ShapeDtypeStruct((t, D_MODEL), F32),
        scratch_shapes=[pltpu.VMEM((EXP_BLK, EXP_TOK), F32),
                        pltpu.VMEM((EXP_BLK, EXP_TOK), BF16),
                        pltpu.VMEM((D_MODEL, EXP_TOK), F32)],
        compiler_params=_cparams(("parallel", "arbitrary"), 58),
        name="experts",
    )(h2, u_b, vt_b, c1, p1, r2, p2, x1, mod3)


def _layer(x, c, ctx, c_ctx, w_ada, b_ada, norm1_w, norm2_w, w_in, q_norm_w, k_norm_w, na_rpb,
           conv_w, conv_b, lru_w_a, lru_b_a, lru_w_x, lru_b_x, lru_lambda, w_o_attn, w_o_lru,
           w_out, peer_w_q, peer_keys, peer_u, peer_v):
    batch, seq, d = x.shape
    n_ctx = ctx.shape[1]
    assert d == D_MODEL and n_ctx == CTX_LEN and seq % (ATT_QROWS * GRID_W) == 0 and batch < MOD_ROWS
    x2 = x.reshape(batch * seq, d)
    ctx2 = ctx.reshape(batch * n_ctx, d)

    c_rows = jnp.zeros((MOD_ROWS, d), F32).at[:batch].set(c).at[batch].set(c_ctx)
    mod3 = _mod(c_rows, w_ada, b_ada).reshape(MOD_ROWS, 1, 6 * d)

    w_in_b = w_in.astype(BF16)
    n1 = norm1_w.reshape(1, d)
    qn_row = (jnp.tile(q_norm_w, NA_HEADS) * (NA_HEAD_DIM ** -0.5)).reshape(1, NA_WIDTH)
    kn_row = jnp.tile(k_norm_w, NA_HEADS).reshape(1, NA_WIDTH)
    qkv, xr, gy, sga, sgb = _inproj(x2, mod3, n1, w_in_b, qn_row, kn_row, seq)
    kv_c, xr_c = _inproj_ctx(ctx2, mod3, n1, w_in_b, kn_row, batch)

    o_a = _attention(qkv, kv_c, _attn_bias_table(na_rpb), batch, seq)

    def gate_params(direction):
        w_gate = jnp.concatenate([lru_w_a[direction], lru_w_x[direction]], axis=-1).astype(BF16)
        b_gate = jnp.stack([lru_b_a[direction], lru_b_x[direction]])
        return w_gate, b_gate, lru_lambda[direction].reshape(1, LRU_WIDTH)

    h_bwd, u, u_c = _lru_pass(xr, xr_c, *gate_params(1), batch, seq, True,
                              conv=(conv_w, conv_b.reshape(1, LRU_WIDTH)))
    o_b = _lru_pass(u, u_c, *gate_params(0), batch, seq, False, h_bwd=h_bwd, gy=gy)

    keys = peer_keys.reshape(2 * PEER_HEADS, PEER_NKEYS, PEER_HALF).astype(BF16)
    x1, h2, st = _merge(o_a, o_b, sga, sgb, x2, mod3, w_o_attn.astype(BF16), w_o_lru.astype(BF16),
                        w_out.astype(BF16), norm2_w.reshape(1, d), peer_w_q.astype(BF16), keys, seq)
    c1, p1, r2, p2 = _select(st)
    out = _experts(h2, peer_u.astype(BF16), peer_v.astype(BF16).T, c1, p1, r2, p2, x1, mod3, seq)
    return out.reshape(batch, seq, d)


def kernel(x, c, ctx, c_ctx, w_ada, b_ada, norm1_w, norm2_w, w_in, q_norm_w, k_norm_w, na_rpb, conv_w, conv_b, lru_w_a, lru_b_a, lru_w_x, lru_b_x, lru_lambda, w_o_attn, w_o_lru, w_out, peer_w_q, peer_keys, peer_u, peer_v):
    depth = w_ada.shape[0]
    for layer in range(depth):
        x = _layer(x, c, ctx, c_ctx, w_ada[layer], b_ada[layer], norm1_w[layer], norm2_w[layer],
                   w_in[layer], q_norm_w[layer], k_norm_w[layer], na_rpb[layer], conv_w[layer],
                   conv_b[layer], lru_w_a[layer], lru_b_a[layer], lru_w_x[layer], lru_b_x[layer],
                   lru_lambda[layer], w_o_attn[layer], w_o_lru[layer], w_out[layer], peer_w_q[layer],
                   peer_keys[layer], peer_u[layer], peer_v[layer])
    return x
```

```python
import functools

import numpy as np
import jax
import jax.numpy as jnp
from jax import lax
from jax.experimental import pallas as pl
from jax.experimental.pallas import tpu as pltpu

F32 = jnp.float32
BF16 = jnp.bfloat16

D_MODEL = 1024
GRID_W = 64
CTX_LEN = 256
NA_HEADS = 8
NA_HEAD_DIM = 64
NA_WIDTH = NA_HEADS * NA_HEAD_DIM
NA_WIN_ROWS = 8
NA_WIN_COLS = 16
LRU_WIDTH = D_MODEL
LRU_BLOCKS = 8
LRU_BLOCK = LRU_WIDTH // LRU_BLOCKS
CONV_WIDTH = 4
CONV_LEFT = 2
LRU_C = 8.0
PEER_HEADS = 8
PEER_NKEYS = 128
PEER_EXPERTS = PEER_NKEYS * PEER_NKEYS
PEER_QDIM = 256
PEER_HALF = PEER_QDIM // 2
PEER_TOPK = 16
EPS = 1e-6
NEG_INF = -1e30

Q0 = 0
K0 = Q0 + NA_WIDTH
V0 = K0 + NA_WIDTH
XR0 = V0 + NA_WIDTH
YR0 = XR0 + LRU_WIDTH
GA0 = YR0 + LRU_WIDTH
GB0 = GA0 + D_MODEL
IN_COLS = GB0 + D_MODEL

V7X_LANES = 128
V7X_SUBLANES = 8
BF16_SUBLANES = 2 * V7X_SUBLANES
V7X_VMEM_BYTES = 64 * 1024 * 1024

MOD_ROWS = 8
ATT_QROWS = 16
ATT_SUB = 4
ATT_KROWS = 12
LRU_CHUNK = 256
HALO = V7X_SUBLANES

_NT = (((1,), (1,)), ((), ()))


def _cparams(sem, vmem_mb, flags=None):
    return pltpu.CompilerParams(dimension_semantics=sem, vmem_limit_bytes=vmem_mb * 1024 * 1024, flags=flags)


def _resident(shape, index_map):
    return pl.BlockSpec(shape, index_map, pipeline_mode=pl.Buffered(1))


def _gelu(x):
    return jax.nn.gelu(x)


def _sigmoid(x):
    return jax.nn.sigmoid(x)


_GELU_A = -2.0 * float(np.sqrt(2.0 / np.pi) * np.log2(np.e))
_GELU_B = _GELU_A * 0.044715


def _gelu_bf16(x):
    xb = x.astype(BF16)
    z = (xb * xb * _GELU_B + _GELU_A) * xb
    return xb / (1.0 + jnp.exp2(z))


def _mod_kernel(c_ref, w_ref, b_ref, o_ref):
    cs = c_ref[...]
    s = cs * _sigmoid(cs)
    o_ref[...] = jnp.dot(s, w_ref[...], preferred_element_type=F32,
                         precision=lax.Precision.HIGHEST) + b_ref[...]


def _mod(c_rows, w_ada, b_ada):
    n = w_ada.shape[1]
    tn = 1536
    return pl.pallas_call(
        _mod_kernel,
        grid=(n // tn,),
        in_specs=[pl.BlockSpec((MOD_ROWS, D_MODEL), lambda j: (0, 0)),
                  pl.BlockSpec((D_MODEL, tn), lambda j: (0, j)),
                  pl.BlockSpec((1, tn), lambda j: (0, j))],
        out_specs=pl.BlockSpec((MOD_ROWS, tn), lambda j: (0, j)),
        out_shape=jax.ShapeDtypeStruct((MOD_ROWS, n), F32),
        compiler_params=_cparams(("arbitrary",), 40),
        name="mod",
    )(c_rows, w_ada, b_ada.reshape(1, n))


def _normed_input(x_ref, mod_ref, n1_ref):
    x = x_ref[...]
    ms = jnp.mean(x * x, axis=-1, keepdims=True)
    h = x * lax.rsqrt(ms + EPS) * n1_ref[...]
    shift = mod_ref[:, 0:D_MODEL]
    scale = mod_ref[:, D_MODEL:2 * D_MODEL]
    return (h * (1.0 + scale) + shift).astype(BF16)


def _head_norm(z, g_ref, w_row):
    zz = z * z
    hi = zz.astype(BF16)
    lo = (zz - hi.astype(F32)).astype(BF16)
    ms = (jnp.dot(hi, g_ref[...], preferred_element_type=F32)
          + jnp.dot(lo, g_ref[...], preferred_element_type=F32))
    return z * lax.rsqrt(ms + EPS) * w_row


def _inproj_kernel(x_ref, mod_ref, n1_ref, w_ref, g_ref, qn_ref, kn_ref,
                   qkv_ref, xr_ref, gy_ref, sga_ref, sgb_ref):
    hb = _normed_input(x_ref, mod_ref, n1_ref)
    seg = lambda lo, hi: jnp.dot(hb, w_ref[:, lo:hi], preferred_element_type=F32)
    qkv_ref[:, Q0:K0] = _head_norm(seg(Q0, K0), g_ref, qn_ref[...]).astype(BF16)
    qkv_ref[:, K0:V0] = _head_norm(seg(K0, V0), g_ref, kn_ref[...]).astype(BF16)
    qkv_ref[:, V0:XR0] = seg(V0, XR0).astype(BF16)
    xr_ref[...] = seg(XR0, YR0)
    gy_ref[...] = _gelu(seg(YR0, GA0))
    sga_ref[...] = _sigmoid(seg(GA0, GB0))
    sgb_ref[...] = _sigmoid(seg(GB0, IN_COLS))


def _inproj_ctx_kernel(x_ref, mod_ref, n1_ref, w_ref, g_ref, kn_ref, kv_ref, xr_ref):
    hb = _normed_input(x_ref, mod_ref, n1_ref)
    seg = lambda lo, hi: jnp.dot(hb, w_ref[:, lo:hi], preferred_element_type=F32)
    kv_ref[:, 0:NA_WIDTH] = _head_norm(seg(K0, V0), g_ref, kn_ref[...]).astype(BF16)
    kv_ref[:, NA_WIDTH:2 * NA_WIDTH] = seg(V0, XR0).astype(BF16)
    xr_ref[...] = seg(XR0, YR0)


def _head_mean_matrix():
    head = np.arange(NA_WIDTH) // NA_HEAD_DIM
    return jnp.asarray((head[:, None] == head[None, :]).astype(np.float32) / NA_HEAD_DIM, dtype=BF16)


def _inproj(x2, mod3, n1, w_in_b, qn_row, kn_row, rows_per_batch):
    t = x2.shape[0]
    tm = 512
    per = rows_per_batch // tm
    row = lambda i: (i, 0)
    const = lambda i: (0, 0)
    wide = pl.BlockSpec((tm, D_MODEL), row)
    return pl.pallas_call(
        _inproj_kernel,
        grid=(t // tm,),
        in_specs=[wide,
                  pl.BlockSpec((None, 1, 6 * D_MODEL), lambda i: (i // per, 0, 0)),
                  _resident((1, D_MODEL), const),
                  _resident((D_MODEL, IN_COLS), const),
                  _resident((NA_WIDTH, NA_WIDTH), const),
                  _resident((1, NA_WIDTH), const),
                  _resident((1, NA_WIDTH), const)],
        out_specs=[pl.BlockSpec((tm, 3 * NA_WIDTH), row), wide, wide, wide, wide],
        out_shape=[jax.ShapeDtypeStruct((t, 3 * NA_WIDTH), BF16)] + [jax.ShapeDtypeStruct((t, D_MODEL), F32)] * 4,
        compiler_params=_cparams(("parallel",), 56),
        name="inproj",
    )(x2, mod3, n1, w_in_b, _head_mean_matrix(), qn_row, kn_row)


def _inproj_ctx(ctx2, mod3, n1, w_in_b, kn_row, ctx_mod_row):
    t = ctx2.shape[0]
    tm = 512
    row = lambda i: (i, 0)
    const = lambda i: (0, 0)
    wide = pl.BlockSpec((tm, D_MODEL), row)
    return pl.pallas_call(
        _inproj_ctx_kernel,
        grid=(t // tm,),
        in_specs=[wide,
                  pl.BlockSpec((None, 1, 6 * D_MODEL), lambda i: (ctx_mod_row, 0, 0)),
                  _resident((1, D_MODEL), const),
                  _resident((D_MODEL, IN_COLS), const),
                  _resident((NA_WIDTH, NA_WIDTH), const),
                  _resident((1, NA_WIDTH), const)],
        out_specs=[wide, wide],
        out_shape=[jax.ShapeDtypeStruct((t, 2 * NA_WIDTH), BF16), jax.ShapeDtypeStruct((t, D_MODEL), F32)],
        compiler_params=_cparams(("parallel",), 56),
        name="inproj_ctx",
    )(ctx2, mod3, n1, w_in_b, _head_mean_matrix(), kn_row)


def _attn_bias_table(rpb):
    wr, wc, w = NA_WIN_ROWS, NA_WIN_COLS, GRID_W
    col = np.arange(w)
    col0 = np.clip(col - wc // 2, 0, w - wc)
    col_mask = (col[None, :] >= col0[:, None]) & (col[None, :] < col0[:, None] + wc)
    dc = np.clip(col[None, :] - col[:, None], -(wc - 1), wc - 1) + wc - 1
    onehot = (dc[None] == np.arange(2 * wc - 1)[:, None, None]).astype(np.float32)
    tab = jnp.einsum('hdc,cqk->hdqk', rpb, onehot, precision=lax.Precision.HIGHEST)
    tab = jnp.where(col_mask[None, None], tab, NEG_INF)
    none = jnp.full((NA_HEADS, 1, w, w), NEG_INF, F32)
    return jnp.concatenate([jnp.concatenate([none, tab], axis=1),
                            jnp.concatenate([tab, none], axis=1)], axis=-1)


def _att_key_start(first_qrow, rows):
    return np.clip(first_qrow - NA_WIN_ROWS // 2, 0, rows - ATT_KROWS)


def _fill_bias(tab_ref, bias_ref, rb, rows):
    wr, w = NA_WIN_ROWS, GRID_W
    lane = lax.broadcasted_iota(jnp.int32, (w, 2 * w), 1)
    for hh in range(V7X_LANES // NA_HEAD_DIM):
        for sub in range(ATT_QROWS // ATT_SUB):
            ks = int(_att_key_start(rb * ATT_QROWS + sub * ATT_SUB, rows))
            for i in range(ATT_SUB):
                r = rb * ATT_QROWS + sub * ATT_SUB + i
                row0 = int(np.clip(r - wr // 2, 0, rows - wr))
                assert ks <= row0 and row0 + wr <= ks + ATT_KROWS
                for w2 in range(ATT_KROWS // 2):
                    kr = ks + 2 * w2
                    left = row0 <= kr < row0 + wr
                    right = row0 <= kr + 1 < row0 + wr
                    if left or right:
                        blk = tab_ref[hh, kr - r + wr]
                        if not right:
                            blk = jnp.where(lane < w, blk, NEG_INF)
                        elif not left:
                            blk = jnp.where(lane >= w, blk, NEG_INF)
                    else:
                        blk = jnp.full((w, 2 * w), NEG_INF, F32)
                    bias_ref[hh, sub, i * w:(i + 1) * w, w2 * 2 * w:(w2 + 1) * 2 * w] = blk


def _attn_kernel(q_ref, k_ref, v_ref, kc_ref, vc_ref, tab_ref, o_ref, bias_ref, *, rows):
    rb = pl.program_id(1)
    n_rb = rows // ATT_QROWS

    for rb_static in (0, 1, n_rb - 1):
        @pl.when(jnp.logical_and(pl.program_id(2) == 0, rb == rb_static))
        def _():
            _fill_bias(tab_ref, bias_ref, rb_static, rows)

    nq, nk = ATT_SUB * GRID_W, ATT_KROWS * GRID_W
    kc = kc_ref[...]
    vc = vc_ref[...]
    lane = lax.broadcasted_iota(jnp.int32, (1, V7X_LANES), 1)
    for sub in range(ATT_QROWS // ATT_SUB):
        first = rb * ATT_QROWS + sub * ATT_SUB
        ks = jnp.clip(first - NA_WIN_ROWS // 2, 0, rows - ATT_KROWS) * GRID_W
        ks = pl.multiple_of(ks, GRID_W)
        q = q_ref[sub * nq:(sub + 1) * nq, :]
        kw = k_ref[pl.ds(ks, nk), :]
        vw = v_ref[pl.ds(ks, nk), :]
        acc = jnp.zeros(q.shape, F32)
        for hh in range(V7X_LANES // NA_HEAD_DIM):
            sel = (lane // NA_HEAD_DIM == hh).astype(BF16)
            qh = q * sel
            s = lax.dot_general(qh, kw, _NT, preferred_element_type=F32) + bias_ref[hh, sub]
            sc = lax.dot_general(qh, kc, _NT, preferred_element_type=F32)
            mx = jnp.maximum(jnp.max(s, axis=-1, keepdims=True), jnp.max(sc, axis=-1, keepdims=True))
            p = jnp.exp(s - mx)
            pc = jnp.exp(sc - mx)
            den = jnp.sum(p, axis=-1, keepdims=True) + jnp.sum(pc, axis=-1, keepdims=True)
            o = (jnp.dot(p.astype(BF16), vw * sel, preferred_element_type=F32)
                 + jnp.dot(pc.astype(BF16), vc * sel, preferred_element_type=F32))
            acc = acc + o / den
        o_ref[sub * nq:(sub + 1) * nq, :] = acc.astype(BF16)


def _attention(qkv, kv_c, tab, batch, seq):
    rows = seq // GRID_W
    n_rb = rows // ATT_QROWS
    tq = ATT_QROWS * GRID_W
    pairs = NA_WIDTH // V7X_LANES
    kcol, vcol = K0 // V7X_LANES, V0 // V7X_LANES
    assert n_rb >= 3
    heads_per_step = V7X_LANES // NA_HEAD_DIM
    return pl.pallas_call(
        functools.partial(_attn_kernel, rows=rows),
        grid=(pairs, n_rb, batch),
        in_specs=[pl.BlockSpec((tq, V7X_LANES), lambda hp, rb, b: (b * n_rb + rb, hp)),
                  pl.BlockSpec((seq, V7X_LANES), lambda hp, rb, b: (b, kcol + hp)),
                  pl.BlockSpec((seq, V7X_LANES), lambda hp, rb, b: (b, vcol + hp)),
                  pl.BlockSpec((CTX_LEN, V7X_LANES), lambda hp, rb, b: (b, hp)),
                  pl.BlockSpec((CTX_LEN, V7X_LANES), lambda hp, rb, b: (b, pairs + hp)),
                  pl.BlockSpec((heads_per_step, 2 * NA_WIN_ROWS, GRID_W, 2 * GRID_W),
                               lambda hp, rb, b: (hp, 0, 0, 0))],
        out_specs=pl.BlockSpec((tq, V7X_LANES), lambda hp, rb, b: (b * n_rb + rb, hp)),
        out_shape=jax.ShapeDtypeStruct((batch * seq, NA_WIDTH), BF16),
        scratch_shapes=[pltpu.VMEM((heads_per_step, ATT_QROWS // ATT_SUB, ATT_SUB * GRID_W, ATT_KROWS * GRID_W),
                                   F32)],
        compiler_params=_cparams(("arbitrary", "arbitrary", "arbitrary"), 48),
        name="attn",
    )(qkv, qkv, qkv, kv_c, kv_c, tab)


def _lru_kernel(*refs, reverse, n_chunks):
    if reverse:
        (x_ref, xc_ref, prev_ref, next_ref, cw_ref, cb_ref, wg_ref, bg_ref, lam_ref,
         h_ref, u_ref, uc_ref, a_ref, b_ref, carry_ref) = refs
    else:
        (u_ref, uc_ref, wg_ref, bg_ref, lam_ref, hb_ref, gy_ref, h_ref, a_ref, b_ref, carry_ref) = refs
    c = pl.program_id(1)
    is_ctx = c == 0
    n = LRU_CHUNK

    if reverse:
        chunk = n_chunks - c
        has_prev = jnp.logical_and(c > 0, chunk > 0)
        has_next = jnp.logical_and(c > 0, chunk < n_chunks - 1)
        g8 = n // HALO
        win = jnp.concatenate([jnp.where(has_prev, prev_ref[...], 0.0)[None],
                               jnp.where(is_ctx, xc_ref[...], x_ref[...]).reshape(g8, HALO, LRU_WIDTH),
                               jnp.where(has_next, next_ref[...], 0.0)[None]], axis=0)
        pos8 = lax.broadcasted_iota(jnp.int32, (1, HALO, 1), 1)
        u = cb_ref[...] + cw_ref[CONV_LEFT:CONV_LEFT + 1, :] * win[1:g8 + 1]
        for j in range(CONV_WIDTH):
            d = j - CONV_LEFT
            if d == 0:
                continue
            rot = pltpu.roll(win, (-d) % HALO, axis=1)
            if d < 0:
                tap = jnp.where(pos8 < -d, rot[0:g8], rot[1:g8 + 1])
            else:
                tap = jnp.where(pos8 >= HALO - d, rot[2:g8 + 2], rot[1:g8 + 1])
            u = u + cw_ref[j:j + 1, :] * tap
        u = u.reshape(n, LRU_WIDTH)
        u_ref[...] = u

        @pl.when(is_ctx)
        def _():
            uc_ref[...] = u
    else:
        u = jnp.where(is_ctx, uc_ref[...], u_ref[...])

    ub = u.astype(BF16)
    lam = lam_ref[...]
    softplus = jnp.maximum(-lam, 0.0) + jnp.log(1.0 + jnp.exp(-jnp.abs(lam)))
    for blk in range(LRU_BLOCKS):
        lo, hi = blk * LRU_BLOCK, (blk + 1) * LRU_BLOCK
        g = jnp.dot(ub[:, lo:hi], wg_ref[blk], preferred_element_type=F32)
        r = _sigmoid(g[:, 0:LRU_BLOCK] + bg_ref[0:1, lo:hi])
        i = _sigmoid(g[:, LRU_BLOCK:2 * LRU_BLOCK] + bg_ref[1:2, lo:hi])
        log_a = -LRU_C * r * softplus[:, lo:hi]
        a = jnp.exp(log_a)
        beta = jnp.sqrt(1.0 - a * a)
        a_ref[:, lo:hi] = a
        b_ref[:, lo:hi] = beta * (i * u[:, lo:hi])

    groups = n // V7X_SUBLANES
    a = a_ref[...].reshape(groups, V7X_SUBLANES, LRU_WIDTH)
    b = b_ref[...].reshape(groups, V7X_SUBLANES, LRU_WIDTH)
    pos = lax.broadcasted_iota(jnp.int32, (1, V7X_SUBLANES, 1), 1)
    for s in (1, 2, 4):
        shift = (V7X_SUBLANES - s) if reverse else s
        a_n = pltpu.roll(a, shift, axis=1)
        b_n = pltpu.roll(b, shift, axis=1)
        ok = (pos < V7X_SUBLANES - s) if reverse else (pos >= s)
        b = jnp.where(ok, a * b_n + b, b)
        a = jnp.where(ok, a * a_n, a)
    a_ref[...] = a.reshape(n, LRU_WIDTH)
    b_ref[...] = b.reshape(n, LRU_WIDTH)

    @pl.when(is_ctx)
    def _():
        carry_ref[...] = jnp.zeros_like(carry_ref)

    edge = 0 if reverse else V7X_SUBLANES - 1

    def group_step(gi, carry):
        g = (groups - 1 - gi) if reverse else gi
        row = pl.multiple_of(g * V7X_SUBLANES, V7X_SUBLANES)
        hg = a_ref[pl.ds(row, V7X_SUBLANES), :] * carry + b_ref[pl.ds(row, V7X_SUBLANES), :]
        if reverse:
            h_ref[pl.ds(row, V7X_SUBLANES), :] = hg
        else:
            h_ref[pl.ds(row, V7X_SUBLANES), :] = ((hg + hb_ref[pl.ds(row, V7X_SUBLANES), :])
                                                  * gy_ref[pl.ds(row, V7X_SUBLANES), :])
        return hg[edge:edge + 1, :]

    carry_ref[...] = lax.fori_loop(0, groups, group_step, carry_ref[...], unroll=True)


def _lru_pass(x_in, xc_in, w_gate, b_gate, lam, batch, seq, reverse, conv=None, h_bwd=None, gy=None):
    n = LRU_CHUNK
    n_chunks = seq // n
    per_halo = n // HALO

    def chunk_of(c):
        return jnp.where(c == 0, (n_chunks - 1) if reverse else 0, (n_chunks - c) if reverse else (c - 1))

    cur = lambda b, c: (b * n_chunks + chunk_of(c), 0)
    prev = lambda b, c: (jnp.maximum((b * n_chunks + chunk_of(c)) * per_halo - 1, 0), 0)
    nxt = lambda b, c: (jnp.minimum((b * n_chunks + chunk_of(c) + 1) * per_halo, batch * seq // HALO - 1), 0)
    const2 = lambda b, c: (0, 0)
    const3 = lambda b, c: (0, 0, 0)
    big = pl.BlockSpec((n, LRU_WIDTH), cur)
    ctx_blk = pl.BlockSpec((n, LRU_WIDTH), lambda b, c: (b, 0))
    gates = [_resident((LRU_BLOCKS, LRU_BLOCK, 2 * LRU_BLOCK), const3),
             _resident((2, LRU_WIDTH), const2),
             _resident((1, LRU_WIDTH), const2)]
    wide = jax.ShapeDtypeStruct((batch * seq, LRU_WIDTH), F32)
    scan_scratch = [pltpu.VMEM((n, LRU_WIDTH), F32), pltpu.VMEM((n, LRU_WIDTH), F32), pltpu.VMEM((1, LRU_WIDTH), F32)]
    if reverse:
        conv_w, conv_b = conv
        in_specs = [big, ctx_blk, pl.BlockSpec((HALO, LRU_WIDTH), prev), pl.BlockSpec((HALO, LRU_WIDTH), nxt),
                    _resident((CONV_WIDTH, LRU_WIDTH), const2), _resident((1, LRU_WIDTH), const2)] + gates
        args = [x_in, xc_in, x_in, x_in, conv_w, conv_b, w_gate, b_gate, lam]
        out_specs = [big, big, ctx_blk]
        out_shape = [wide, wide, jax.ShapeDtypeStruct(xc_in.shape, F32)]
        scratch = scan_scratch
    else:
        in_specs = [big, ctx_blk] + gates + [big, big]
        args = [x_in, xc_in, w_gate, b_gate, lam, h_bwd, gy]
        out_specs = big
        out_shape = wide
        scratch = scan_scratch
    return pl.pallas_call(
        functools.partial(_lru_kernel, reverse=reverse, n_chunks=n_chunks),
        grid=(batch, n_chunks + 1),
        in_specs=in_specs,
        out_specs=out_specs,
        out_shape=out_shape,
        scratch_shapes=scratch,
        compiler_params=_cparams(("parallel", "arbitrary"), 40),
        name="lru_bwd" if reverse else "lru_fwd",
    )(*args)


def _merge_kernel(oa_ref, ob_ref, sga_ref, sgb_ref, x_ref, mod_ref, woa_ref, wol_ref, wout_ref,
                  n2_ref, wq_ref, keys_ref, x1_ref, h2_ref, st_ref):
    ma = jnp.dot(oa_ref[...], woa_ref[...], preferred_element_type=F32)
    mb = jnp.dot(ob_ref[...].astype(BF16), wol_ref[...], preferred_element_type=F32)
    merged = sga_ref[...] * ma + sgb_ref[...] * mb
    gate1 = mod_ref[:, 2 * D_MODEL:3 * D_MODEL]
    x1 = x_ref[...] + gate1 * jnp.dot(merged.astype(BF16), wout_ref[...], preferred_element_type=F32)
    x1_ref[...] = x1
    ms = jnp.mean(x1 * x1, axis=-1, keepdims=True)
    h = x1 * lax.rsqrt(ms + EPS) * n2_ref[...]
    shift2 = mod_ref[:, 3 * D_MODEL:4 * D_MODEL]
    scale2 = mod_ref[:, 4 * D_MODEL:5 * D_MODEL]
    h2 = (h * (1.0 + scale2) + shift2).astype(BF16)
    h2_ref[...] = h2
    qp = jnp.dot(h2, wq_ref[...], preferred_element_type=F32).astype(BF16)
    for hp in range(2 * PEER_HEADS):
        st_ref[hp] = lax.dot_general(keys_ref[hp], qp[:, hp * PEER_HALF:(hp + 1) * PEER_HALF], _NT,
                                     preferred_element_type=F32)


def _merge(o_a, o_b, sga, sgb, x2, mod3, w_oa, w_ol, w_out, n2, w_q, keys, rows_per_batch):
    t = x2.shape[0]
    tm = 512
    per = rows_per_batch // tm
    row = lambda i: (i, 0)
    const = lambda i: (0, 0)
    wide = pl.BlockSpec((tm, D_MODEL), row)
    nq = PEER_HEADS * PEER_QDIM
    return pl.pallas_call(
        _merge_kernel,
        grid=(t // tm,),
        in_specs=[pl.BlockSpec((tm, NA_WIDTH), row), wide, wide, wide, wide,
                  pl.BlockSpec((None, 1, 6 * D_MODEL), lambda i: (i // per, 0, 0)),
                  _resident((NA_WIDTH, D_MODEL), const),
                  _resident((LRU_WIDTH, D_MODEL), const),
                  _resident((D_MODEL, D_MODEL), const),
                  _resident((1, D_MODEL), const),
                  _resident((D_MODEL, nq), const),
                  _resident((2 * PEER_HEADS, PEER_NKEYS, PEER_HALF), lambda i: (0, 0, 0))],
        out_specs=[wide, wide, pl.BlockSpec((2 * PEER_HEADS, PEER_NKEYS, tm), lambda i: (0, 0, i))],
        out_shape=[jax.ShapeDtypeStruct((t, D_MODEL), F32),
                   jax.ShapeDtypeStruct((t, D_MODEL), BF16),
                   jax.ShapeDtypeStruct((2 * PEER_HEADS, PEER_NKEYS, t), F32)],
        compiler_params=_cparams(("parallel",), 56),
        name="merge",
    )(o_a, o_b, sga, sgb, x2, mod3, w_oa, w_ol, w_out, n2, w_q, keys)


def _top_ranked(x, k, with_rank=True):
    rank = jnp.full(x.shape, float(k), F32) if with_rank else None
    vals = []
    for b in range(k):
        m = jnp.max(x, axis=0, keepdims=True)
        hit = x >= m
        if with_rank:
            rank = jnp.where(hit, float(b), rank)
        x = jnp.where(hit, NEG_INF, x)
        vals.append(m)
    return vals, rank


def _stack_rows(rows):
    cols = rows[0].shape[1]
    slot = lax.broadcasted_iota(jnp.int32, (len(rows), cols), 0)
    out = jnp.zeros((len(rows), cols), F32)
    for b, row in enumerate(rows):
        out = jnp.where(slot == b, row, out)
    return out


_PAIR_LIMIT = [PEER_TOPK // (a + 1) for a in range(PEER_TOPK)]
_PAIR_HALF = PEER_TOPK // 2


def _select_kernel(st_ref, c1_ref, p1_ref, r2_ref, p2_ref):
    k = PEER_TOPK
    cols = st_ref.shape[2]

    slot = lax.broadcasted_iota(jnp.int32, (_PAIR_HALF, cols), 0)
    for h in range(PEER_HEADS):
        s1 = st_ref[2 * h]
        s2 = st_ref[2 * h + 1]
        v1, _ = _top_ranked(s1, k, with_rank=False)
        v2, rank2 = _top_ranked(s2, k)
        v2_all = _stack_rows(v2)
        groups = [v1[0] + v2_all]
        for a in range(1, _PAIR_HALF):
            groups.append(jnp.where(slot < _PAIR_LIMIT[a], v1[a] + v2_all[0:_PAIR_HALF], NEG_INF))
        groups.append(_stack_rows(v1[_PAIR_HALF:]) + v2[0])
        cand0 = jnp.concatenate(groups, axis=0)
        cand = cand0
        tau = jnp.full_like(v1[0], NEG_INF)
        seen = jnp.zeros_like(v1[0])
        for _ in range(k):
            m = jnp.max(cand, axis=0, keepdims=True)
            hit = cand >= m
            tau = jnp.where(seen < k, m, tau)
            seen = seen + jnp.sum(hit.astype(F32), axis=0, keepdims=True)
            cand = jnp.where(hit, NEG_INF, cand)
        picked = cand0 >= tau
        z = jnp.sum(jnp.where(picked, jnp.exp(cand0 - (v1[0] + v2[0])), 0.0), axis=0, keepdims=True)
        pf = picked.astype(F32)
        counts = [jnp.sum(pf[0:k], axis=0, keepdims=True)]
        for a in range(1, _PAIR_HALF):
            lo = k + (a - 1) * _PAIR_HALF
            counts.append(jnp.sum(pf[lo:lo + _PAIR_HALF], axis=0, keepdims=True))
        lo = k + (_PAIR_HALF - 1) * _PAIR_HALF
        counts += [pf[lo + a:lo + a + 1] for a in range(k - _PAIR_HALF)]
        c1 = jnp.zeros_like(s1)
        for a in range(k):
            c1 = jnp.where(s1 == v1[a], counts[a], c1)
        c1_ref[h] = c1
        p1_ref[h] = jnp.exp(s1 - v1[0])
        r2_ref[h * PEER_NKEYS:(h + 1) * PEER_NKEYS, :] = rank2.astype(BF16)
        p2_ref[h * PEER_NKEYS:(h + 1) * PEER_NKEYS, :] = (jnp.exp(s2 - v2[0]) / z).astype(BF16)


def _select(st):
    t = st.shape[2]
    tt = 256
    blk = pl.BlockSpec((PEER_HEADS, PEER_NKEYS, tt), lambda i: (0, 0, i))
    flat = pl.BlockSpec((PEER_HEADS * PEER_NKEYS, tt), lambda i: (0, i))
    full = jax.ShapeDtypeStruct((PEER_HEADS, PEER_NKEYS, t), F32)
    full_flat = jax.ShapeDtypeStruct((PEER_HEADS * PEER_NKEYS, t), BF16)
    return pl.pallas_call(
        _select_kernel,
        grid=(t // tt,),
        in_specs=[pl.BlockSpec((2 * PEER_HEADS, PEER_NKEYS, tt), lambda i: (0, 0, i))],
        out_specs=[blk, blk, flat, flat],
        out_shape=[full, full, full_flat, full_flat],
        compiler_params=_cparams(("parallel",), 40),
        name="select",
    )(st)


EXP_TOK = 512
EXP_BLK = 2048
EXP_I = EXP_BLK // PEER_NKEYS
GATE_ROWS = V7X_SUBLANES
GATE_LANES = 2 * V7X_LANES


def _gate_times_act(at_ref, m_ref, c1_ref, p1_ref, r2_ref, p2_ref):
    jt, jl = GATE_ROWS, GATE_LANES
    for lt in range(EXP_TOK // jl):
        lanes = slice(lt * jl, (lt + 1) * jl)
        c1_rows = [c1_ref[h, :, lanes] for h in range(PEER_HEADS)]
        p1_rows = [p1_ref[h, :, lanes] for h in range(PEER_HEADS)]
        for il in range(EXP_I):
            w = [None] * (PEER_NKEYS // jt)
            for h in range(PEER_HEADS):
                c1 = jnp.broadcast_to(c1_rows[h][il:il + 1, :], (jt, jl)).astype(BF16)
                p1 = jnp.broadcast_to(p1_rows[h][il:il + 1, :], (jt, jl)).astype(BF16)
                for jb in range(PEER_NKEYS // jt):
                    js = slice(h * PEER_NKEYS + jb * jt, h * PEER_NKEYS + (jb + 1) * jt)
                    term = jnp.where(r2_ref[js, lanes] < c1, p2_ref[js, lanes] * p1, 0)
                    w[jb] = term if h == 0 else w[jb] + term
            row0 = il * PEER_NKEYS
            m_ref[row0:row0 + PEER_NKEYS, lanes] = _gelu_bf16(at_ref[row0:row0 + PEER_NKEYS, lanes])
            for jb in range(PEER_NKEYS // jt):
                ms = slice(row0 + jb * jt, row0 + (jb + 1) * jt)
                m_ref[ms, lanes] = w[jb] * m_ref[ms, lanes]


def _experts_kernel(h2_ref, u_ref, vt_ref, c1_ref, p1_ref, r2_ref, p2_ref, x1_ref, mod_ref,
                    o_ref, at_ref, m_ref, yt_ref):
    eb = pl.program_id(1)

    @pl.when(eb == 0)
    def _():
        yt_ref[...] = jnp.zeros_like(yt_ref)

    at_ref[...] = lax.dot_general(u_ref[...], h2_ref[...], _NT, preferred_element_type=F32)
    _gate_times_act(at_ref, m_ref, c1_ref, p1_ref, r2_ref, p2_ref)
    yt_ref[...] += jnp.dot(vt_ref[...], m_ref[...], preferred_element_type=F32)

    @pl.when(eb == pl.num_programs(1) - 1)
    def _():
        gate2 = mod_ref[:, 5 * D_MODEL:6 * D_MODEL]
        o_ref[...] = x1_ref[...] + gate2 * yt_ref[...].T


def _experts(h2, u_b, vt_b, c1, p1, r2, p2, x1, mod3, rows_per_batch):
    t = h2.shape[0]
    per = rows_per_batch // EXP_TOK
    tok = lambda i, e: (i, 0)
    rows = pl.BlockSpec((PEER_HEADS, EXP_I, EXP_TOK), lambda i, e: (0, e, i))
    sel_flat = pl.BlockSpec((PEER_HEADS * PEER_NKEYS, EXP_TOK), lambda i, e: (0, i))
    return pl.pallas_call(
        _experts_kernel,
        grid=(t // EXP_TOK, PEER_EXPERTS // EXP_BLK),
        in_specs=[pl.BlockSpec((EXP_TOK, D_MODEL), tok),
                  pl.BlockSpec((EXP_BLK, D_MODEL), lambda i, e: (e, 0)),
                  pl.BlockSpec((D_MODEL, EXP_BLK), lambda i, e: (0, e)),
                  rows, rows, sel_flat, sel_flat,
                  pl.BlockSpec((EXP_TOK, D_MODEL), tok),
                  pl.BlockSpec((None, 1, 6 * D_MODEL), lambda i, e: (i // per, 0, 0))],
        out_specs=pl.BlockSpec((EXP_TOK, D_MODEL), tok),
        out_shape=jax.ShapeDtypeStruct((t, D_MODEL), F32),
        scratch_shapes=[pltpu.VMEM((EXP_BLK, EXP_TOK), F32),
                        pltpu.VMEM((EXP_BLK, EXP_TOK), BF16),
                        pltpu.VMEM((D_MODEL, EXP_TOK), F32)],
        compiler_params=_cparams(("parallel", "arbitrary"), 58),
        name="experts",
    )(h2, u_b, vt_b, c1, p1, r2, p2, x1, mod3)


def _layer(x, c, ctx, c_ctx, w_ada, b_ada, norm1_w, norm2_w, w_in, q_norm_w, k_norm_w, na_rpb,
           conv_w, conv_b, lru_w_a, lru_b_a, lru_w_x, lru_b_x, lru_lambda, w_o_attn, w_o_lru,
           w_out, peer_w_q, peer_keys, peer_u, peer_v):
    batch, seq, d = x.shape
    n_ctx = ctx.shape[1]
    assert d == D_MODEL and n_ctx == CTX_LEN and seq % (ATT_QROWS * GRID_W) == 0 and batch < MOD_ROWS
    x2 = x.reshape(batch * seq, d)
    ctx2 = ctx.reshape(batch * n_ctx, d)

    c_rows = jnp.zeros((MOD_ROWS, d), F32).at[:batch].set(c).at[batch].set(c_ctx)
    mod3 = _mod(c_rows, w_ada, b_ada).reshape(MOD_ROWS, 1, 6 * d)

    w_in_b = w_in.astype(BF16)
    n1 = norm1_w.reshape(1, d)
    qn_row = (jnp.tile(q_norm_w, NA_HEADS) * (NA_HEAD_DIM ** -0.5)).reshape(1, NA_WIDTH)
    kn_row = jnp.tile(k_norm_w, NA_HEADS).reshape(1, NA_WIDTH)
    qkv, xr, gy, sga, sgb = _inproj(x2, mod3, n1, w_in_b, qn_row, kn_row, seq)
    kv_c, xr_c = _inproj_ctx(ctx2, mod3, n1, w_in_b, kn_row, batch)

    o_a = _attention(qkv, kv_c, _attn_bias_table(na_rpb), batch, seq)

    def gate_params(direction):
        w_gate = jnp.concatenate([lru_w_a[direction], lru_w_x[direction]], axis=-1).astype(BF16)
        b_gate = jnp.stack([lru_b_a[direction], lru_b_x[direction]])
        return w_gate, b_gate, lru_lambda[direction].reshape(1, LRU_WIDTH)

    h_bwd, u, u_c = _lru_pass(xr, xr_c, *gate_params(1), batch, seq, True,
                              conv=(conv_w, conv_b.reshape(1, LRU_WIDTH)))
    o_b = _lru_pass(u, u_c, *gate_params(0), batch, seq, False, h_bwd=h_bwd, gy=gy)

    keys = peer_keys.reshape(2 * PEER_HEADS, PEER_NKEYS, PEER_HALF).astype(BF16)
    x1, h2, st = _merge(o_a, o_b, sga, sgb, x2, mod3, w_o_attn.astype(BF16), w_o_lru.astype(BF16),
                        w_out.astype(BF16), norm2_w.reshape(1, d), peer_w_q.astype(BF16), keys, seq)
    c1, p1, r2, p2 = _select(st)
    out = _experts(h2, peer_u.astype(BF16), peer_v.astype(BF16).T, c1, p1, r2, p2, x1, mod3, seq)
    return out.reshape(batch, seq, d)


def kernel(x, c, ctx, c_ctx, w_ada, b_ada, norm1_w, norm2_w, w_in, q_norm_w, k_norm_w, na_rpb, conv_w, conv_b, lru_w_a, lru_b_a, lru_w_x, lru_b_x, lru_lambda, w_o_attn, w_o_lru, w_out, peer_w_q, peer_keys, peer_u, peer_v):
    depth = w_ada.shape[0]
    for layer in range(depth):
        x = _layer(x, c, ctx, c_ctx, w_ada[layer], b_ada[layer], norm1_w[layer], norm2_w[layer],
                   w_in[layer], q_norm_w[layer], k_norm_w[layer], na_rpb[layer], conv_w[layer],
                   conv_b[layer], lru_w_a[layer], lru_b_a[layer], lru_w_x[layer], lru_b_x[layer],
                   lru_lambda[layer], w_o_attn[layer], w_o_lru[layer], w_out[layer], peer_w_q[layer],
                   peer_keys[layer], peer_u[layer], peer_v[layer])
    return x
```

```python
import functools

import numpy as np
import jax
import jax.numpy as jnp
from jax import lax
from jax.experimental import pallas as pl
from jax.experimental.pallas import tpu as pltpu

F32 = jnp.float32
BF16 = jnp.bfloat16

D_MODEL = 1024
GRID_W = 64
CTX_LEN = 256
NA_HEADS = 8
NA_HEAD_DIM = 64
NA_WIDTH = NA_HEADS * NA_HEAD_DIM
NA_WIN_ROWS = 8
NA_WIN_COLS = 16
LRU_WIDTH = D_MODEL
LRU_BLOCKS = 8
LRU_BLOCK = LRU_WIDTH // LRU_BLOCKS
CONV_WIDTH = 4
CONV_LEFT = 2
LRU_C = 8.0
PEER_HEADS = 8
PEER_NKEYS = 128
PEER_EXPERTS = PEER_NKEYS * PEER_NKEYS
PEER_QDIM = 256
PEER_HALF = PEER_QDIM // 2
PEER_TOPK = 16
EPS = 1e-6
NEG_INF = -1e30

Q0 = 0
K0 = Q0 + NA_WIDTH
V0 = K0 + NA_WIDTH
XR0 = V0 + NA_WIDTH
YR0 = XR0 + LRU_WIDTH
GA0 = YR0 + LRU_WIDTH
GB0 = GA0 + D_MODEL
IN_COLS = GB0 + D_MODEL

V7X_LANES = 128
V7X_SUBLANES = 8
BF16_SUBLANES = 2 * V7X_SUBLANES
V7X_VMEM_BYTES = 64 * 1024 * 1024

MOD_ROWS = 8
ATT_QROWS = 16
ATT_SUB = 4
ATT_KROWS = 12
LRU_CHUNK = 256
HALO = V7X_SUBLANES

_NT = (((1,), (1,)), ((), ()))


def _cparams(sem, vmem_mb, flags=None):
    return pltpu.CompilerParams(dimension_semantics=sem, vmem_limit_bytes=vmem_mb * 1024 * 1024, flags=flags)


def _resident(shape, index_map):
    return pl.BlockSpec(shape, index_map, pipeline_mode=pl.Buffered(1))


def _gelu(x):
    return jax.nn.gelu(x)


def _sigmoid(x):
    return jax.nn.sigmoid(x)


_GELU_A = -2.0 * float(np.sqrt(2.0 / np.pi) * np.log2(np.e))
_GELU_B = _GELU_A * 0.044715


def _gelu_bf16(x):
    xb = x.astype(BF16)
    z = (xb * xb * _GELU_B + _GELU_A) * xb
    return xb / (1.0 + jnp.exp2(z))


def _mod_kernel(c_ref, w_ref, b_ref, o_ref):
    cs = c_ref[...]
    s = cs * _sigmoid(cs)
    o_ref[...] = jnp.dot(s, w_ref[...], preferred_element_type=F32,
                         precision=lax.Precision.HIGHEST) + b_ref[...]


def _mod(c_rows, w_ada, b_ada):
    n = w_ada.shape[1]
    tn = 1536
    return pl.pallas_call(
        _mod_kernel,
        grid=(n // tn,),
        in_specs=[pl.BlockSpec((MOD_ROWS, D_MODEL), lambda j: (0, 0)),
                  pl.BlockSpec((D_MODEL, tn), lambda j: (0, j)),
                  pl.BlockSpec((1, tn), lambda j: (0, j))],
        out_specs=pl.BlockSpec((MOD_ROWS, tn), lambda j: (0, j)),
        out_shape=jax.ShapeDtypeStruct((MOD_ROWS, n), F32),
        compiler_params=_cparams(("arbitrary",), 40),
        name="mod",
    )(c_rows, w_ada, b_ada.reshape(1, n))


def _normed_input(x_ref, mod_ref, n1_ref):
    x = x_ref[...]
    ms = jnp.mean(x * x, axis=-1, keepdims=True)
    h = x * lax.rsqrt(ms + EPS) * n1_ref[...]
    shift = mod_ref[:, 0:D_MODEL]
    scale = mod_ref[:, D_MODEL:2 * D_MODEL]
    return (h * (1.0 + scale) + shift).astype(BF16)


def _head_norm(z, g_ref, w_row):
    zz = z * z
    hi = zz.astype(BF16)
    lo = (zz - hi.astype(F32)).astype(BF16)
    ms = (jnp.dot(hi, g_ref[...], preferred_element_type=F32)
          + jnp.dot(lo, g_ref[...], preferred_element_type=F32))
    return z * lax.rsqrt(ms + EPS) * w_row


def _inproj_kernel(x_ref, mod_ref, n1_ref, w_ref, g_ref, qn_ref, kn_ref,
                   qkv_ref, xr_ref, gy_ref, sga_ref, sgb_ref):
    hb = _normed_input(x_ref, mod_ref, n1_ref)
    seg = lambda lo, hi: jnp.dot(hb, w_ref[:, lo:hi], preferred_element_type=F32)
    qkv_ref[:, Q0:K0] = _head_norm(seg(Q0, K0), g_ref, qn_ref[...]).astype(BF16)
    qkv_ref[:, K0:V0] = _head_norm(seg(K0, V0), g_ref, kn_ref[...]).astype(BF16)
    qkv_ref[:, V0:XR0] = seg(V0, XR0).astype(BF16)
    xr_ref[...] = seg(XR0, YR0)
    gy_ref[...] = _gelu(seg(YR0, GA0))
    sga_ref[...] = _sigmoid(seg(GA0, GB0))
    sgb_ref[...] = _sigmoid(seg(GB0, IN_COLS))


def _inproj_ctx_kernel(x_ref, mod_ref, n1_ref, w_ref, g_ref, kn_ref, kv_ref, xr_ref):
    hb = _normed_input(x_ref, mod_ref, n1_ref)
    seg = lambda lo, hi: jnp.dot(hb, w_ref[:, lo:hi], preferred_element_type=F32)
    kv_ref[:, 0:NA_WIDTH] = _head_norm(seg(K0, V0), g_ref, kn_ref[...]).astype(BF16)
    kv_ref[:, NA_WIDTH:2 * NA_WIDTH] = seg(V0, XR0).astype(BF16)
    xr_ref[...] = seg(XR0, YR0)


def _head_mean_matrix():
    head = np.arange(NA_WIDTH) // NA_HEAD_DIM
    return jnp.asarray((head[:, None] == head[None, :]).astype(np.float32) / NA_HEAD_DIM, dtype=BF16)


def _inproj(x2, mod3, n1, w_in_b, qn_row, kn_row, rows_per_batch):
    t = x2.shape[0]
    tm = 512
    per = rows_per_batch // tm
    row = lambda i: (i, 0)
    const = lambda i: (0, 0)
    wide = pl.BlockSpec((tm, D_MODEL), row)
    return pl.pallas_call(
        _inproj_kernel,
        grid=(t // tm,),
        in_specs=[wide,
                  pl.BlockSpec((None, 1, 6 * D_MODEL), lambda i: (i // per, 0, 0)),
                  _resident((1, D_MODEL), const),
                  _resident((D_MODEL, IN_COLS), const),
                  _resident((NA_WIDTH, NA_WIDTH), const),
                  _resident((1, NA_WIDTH), const),
                  _resident((1, NA_WIDTH), const)],
        out_specs=[pl.BlockSpec((tm, 3 * NA_WIDTH), row), wide, wide, wide, wide],
        out_shape=[jax.ShapeDtypeStruct((t, 3 * NA_WIDTH), BF16)] + [jax.ShapeDtypeStruct((t, D_MODEL), F32)] * 4,
        compiler_params=_cparams(("parallel",), 56),
        name="inproj",
    )(x2, mod3, n1, w_in_b, _head_mean_matrix(), qn_row, kn_row)


def _inproj_ctx(ctx2, mod3, n1, w_in_b, kn_row, ctx_mod_row):
    t = ctx2.shape[0]
    tm = 512
    row = lambda i: (i, 0)
    const = lambda i: (0, 0)
    wide = pl.BlockSpec((tm, D_MODEL), row)
    return pl.pallas_call(
        _inproj_ctx_kernel,
        grid=(t // tm,),
        in_specs=[wide,
                  pl.BlockSpec((None, 1, 6 * D_MODEL), lambda i: (ctx_mod_row, 0, 0)),
                  _resident((1, D_MODEL), const),
                  _resident((D_MODEL, IN_COLS), const),
                  _resident((NA_WIDTH, NA_WIDTH), const),
                  _resident((1, NA_WIDTH), const)],
        out_specs=[wide, wide],
        out_shape=[jax.ShapeDtypeStruct((t, 2 * NA_WIDTH), BF16), jax.ShapeDtypeStruct((t, D_MODEL), F32)],
        compiler_params=_cparams(("parallel",), 56),
        name="inproj_ctx",
    )(ctx2, mod3, n1, w_in_b, _head_mean_matrix(), kn_row)


def _attn_bias_table(rpb):
    wr, wc, w = NA_WIN_ROWS, NA_WIN_COLS, GRID_W
    col = np.arange(w)
    col0 = np.clip(col - wc // 2, 0, w - wc)
    col_mask = (col[None, :] >= col0[:, None]) & (col[None, :] < col0[:, None] + wc)
    dc = np.clip(col[None, :] - col[:, None], -(wc - 1), wc - 1) + wc - 1
    onehot = (dc[None] == np.arange(2 * wc - 1)[:, None, None]).astype(np.float32)
    tab = jnp.einsum('hdc,cqk->hdqk', rpb, onehot, precision=lax.Precision.HIGHEST)
    tab = jnp.where(col_mask[None, None], tab, NEG_INF)
    none = jnp.full((NA_HEADS, 1, w, w), NEG_INF, F32)
    return jnp.concatenate([jnp.concatenate([none, tab], axis=1),
                            jnp.concatenate([tab, none], axis=1)], axis=-1)


def _att_key_start(first_qrow, rows):
    return np.clip(first_qrow - NA_WIN_ROWS // 2, 0, rows - ATT_KROWS)


def _fill_bias(tab_ref, bias_ref, rb, rows):
    wr, w = NA_WIN_ROWS, GRID_W
    lane = lax.broadcasted_iota(jnp.int32, (w, 2 * w), 1)
    for hh in range(V7X_LANES // NA_HEAD_DIM):
        for sub in range(ATT_QROWS // ATT_SUB):
            ks = int(_att_key_start(rb * ATT_QROWS + sub * ATT_SUB, rows))
            for i in range(ATT_SUB):
                r = rb * ATT_QROWS + sub * ATT_SUB + i
                row0 = int(np.clip(r - wr // 2, 0, rows - wr))
                assert ks <= row0 and row0 + wr <= ks + ATT_KROWS
                for w2 in range(ATT_KROWS // 2):
                    kr = ks + 2 * w2
                    left = row0 <= kr < row0 + wr
                    right = row0 <= kr + 1 < row0 + wr
                    if left or right:
                        blk = tab_ref[hh, kr - r + wr]
                        if not right:
                            blk = jnp.where(lane < w, blk, NEG_INF)
                        elif not left:
                            blk = jnp.where(lane >= w, blk, NEG_INF)
                    else:
                        blk = jnp.full((w, 2 * w), NEG_INF, F32)
                    bias_ref[hh, sub, i * w:(i + 1) * w, w2 * 2 * w:(w2 + 1) * 2 * w] = blk


def _attn_kernel(q_ref, k_ref, v_ref, kc_ref, vc_ref, tab_ref, o_ref, bias_ref, *, rows):
    rb = pl.program_id(1)
    n_rb = rows // ATT_QROWS

    for rb_static in (0, 1, n_rb - 1):
        @pl.when(jnp.logical_and(pl.program_id(2) == 0, rb == rb_static))
        def _():
            _fill_bias(tab_ref, bias_ref, rb_static, rows)

    nq, nk = ATT_SUB * GRID_W, ATT_KROWS * GRID_W
    kc = kc_ref[...]
    vc = vc_ref[...]
    lane = lax.broadcasted_iota(jnp.int32, (1, V7X_LANES), 1)
    for sub in range(ATT_QROWS // ATT_SUB):
        first = rb * ATT_QROWS + sub * ATT_SUB
        ks = jnp.clip(first - NA_WIN_ROWS // 2, 0, rows - ATT_KROWS) * GRID_W
        ks = pl.multiple_of(ks, GRID_W)
        q = q_ref[sub * nq:(sub + 1) * nq, :]
        kw = k_ref[pl.ds(ks, nk), :]
        vw = v_ref[pl.ds(ks, nk), :]
        acc = jnp.zeros(q.shape, F32)
        for hh in range(V7X_LANES // NA_HEAD_DIM):
            sel = (lane // NA_HEAD_DIM == hh).astype(BF16)
            qh = q * sel
            s = lax.dot_general(qh, kw, _NT, preferred_element_type=F32) + bias_ref[hh, sub]
            sc = lax.dot_general(qh, kc, _NT, preferred_element_type=F32)
            mx = jnp.maximum(jnp.max(s, axis=-1, keepdims=True), jnp.max(sc, axis=-1, keepdims=True))
            p = jnp.exp(s - mx)
            pc = jnp.exp(sc - mx)
            den = jnp.sum(p, axis=-1, keepdims=True) + jnp.sum(pc, axis=-1, keepdims=True)
            o = (jnp.dot(p.astype(BF16), vw * sel, preferred_element_type=F32)
                 + jnp.dot(pc.astype(BF16), vc * sel, preferred_element_type=F32))
            acc = acc + o / den
        o_ref[sub * nq:(sub + 1) * nq, :] = acc.astype(BF16)


def _attention(qkv, kv_c, tab, batch, seq):
    rows = seq // GRID_W
    n_rb = rows // ATT_QROWS
    tq = ATT_QROWS * GRID_W
    pairs = NA_WIDTH // V7X_LANES
    kcol, vcol = K0 // V7X_LANES, V0 // V7X_LANES
    assert n_rb >= 3
    heads_per_step = V7X_LANES // NA_HEAD_DIM
    return pl.pallas_call(
        functools.partial(_attn_kernel, rows=rows),
        grid=(pairs, n_rb, batch),
        in_specs=[pl.BlockSpec((tq, V7X_LANES), lambda hp, rb, b: (b * n_rb + rb, hp)),
                  pl.BlockSpec((seq, V7X_LANES), lambda hp, rb, b: (b, kcol + hp)),
                  pl.BlockSpec((seq, V7X_LANES), lambda hp, rb, b: (b, vcol + hp)),
                  pl.BlockSpec((CTX_LEN, V7X_LANES), lambda hp, rb, b: (b, hp)),
                  pl.BlockSpec((CTX_LEN, V7X_LANES), lambda hp, rb, b: (b, pairs + hp)),
                  pl.BlockSpec((heads_per_step, 2 * NA_WIN_ROWS, GRID_W, 2 * GRID_W),
                               lambda hp, rb, b: (hp, 0, 0, 0))],
        out_specs=pl.BlockSpec((tq, V7X_LANES), lambda hp, rb, b: (b * n_rb + rb, hp)),
        out_shape=jax.ShapeDtypeStruct((batch * seq, NA_WIDTH), BF16),
        scratch_shapes=[pltpu.VMEM((heads_per_step, ATT_QROWS // ATT_SUB, ATT_SUB * GRID_W, ATT_KROWS * GRID_W),
                                   F32)],
        compiler_params=_cparams(("arbitrary", "arbitrary", "arbitrary"), 48),
        name="attn",
    )(qkv, qkv, qkv, kv_c, kv_c, tab)


def _lru_kernel(*refs, reverse, n_chunks):
    if reverse:
        (x_ref, xc_ref, prev_ref, next_ref, cw_ref, cb_ref, wg_ref, bg_ref, lam_ref,
         h_ref, u_ref, uc_ref, a_ref, b_ref, carry_ref) = refs
    else:
        (u_ref, uc_ref, wg_ref, bg_ref, lam_ref, hb_ref, gy_ref, h_ref, a_ref, b_ref, carry_ref) = refs
    c = pl.program_id(1)
    is_ctx = c == 0
    n = LRU_CHUNK

    if reverse:
        chunk = n_chunks - c
        has_prev = jnp.logical_and(c > 0, chunk > 0)
        has_next = jnp.logical_and(c > 0, chunk < n_chunks - 1)
        g8 = n // HALO
        win = jnp.concatenate([jnp.where(has_prev, prev_ref[...], 0.0)[None],
                               jnp.where(is_ctx, xc_ref[...], x_ref[...]).reshape(g8, HALO, LRU_WIDTH),
                               jnp.where(has_next, next_ref[...], 0.0)[None]], axis=0)
        pos8 = lax.broadcasted_iota(jnp.int32, (1, HALO, 1), 1)
        u = cb_ref[...] + cw_ref[CONV_LEFT:CONV_LEFT + 1, :] * win[1:g8 + 1]
        for j in range(CONV_WIDTH):
            d = j - CONV_LEFT
            if d == 0:
                continue
            rot = pltpu.roll(win, (-d) % HALO, axis=1)
            if d < 0:
                tap = jnp.where(pos8 < -d, rot[0:g8], rot[1:g8 + 1])
            else:
                tap = jnp.where(pos8 >= HALO - d, rot[2:g8 + 2], rot[1:g8 + 1])
            u = u + cw_ref[j:j + 1, :] * tap
        u = u.reshape(n, LRU_WIDTH)
        u_ref[...] = u

        @pl.when(is_ctx)
        def _():
            uc_ref[...] = u
    else:
        u = jnp.where(is_ctx, uc_ref[...], u_ref[...])

    ub = u.astype(BF16)
    lam = lam_ref[...]
    softplus = jnp.maximum(-lam, 0.0) + jnp.log(1.0 + jnp.exp(-jnp.abs(lam)))
    for blk in range(LRU_BLOCKS):
        lo, hi = blk * LRU_BLOCK, (blk + 1) * LRU_BLOCK
        g = jnp.dot(ub[:, lo:hi], wg_ref[blk], preferred_element_type=F32)
        r = _sigmoid(g[:, 0:LRU_BLOCK] + bg_ref[0:1, lo:hi])
        i = _sigmoid(g[:, LRU_BLOCK:2 * LRU_BLOCK] + bg_ref[1:2, lo:hi])
        log_a = -LRU_C * r * softplus[:, lo:hi]
        a = jnp.exp(log_a)
        beta = jnp.sqrt(1.0 - a * a)
        a_ref[:, lo:hi] = a
        b_ref[:, lo:hi] = beta * (i * u[:, lo:hi])

    groups = n // V7X_SUBLANES
    a = a_ref[...].reshape(groups, V7X_SUBLANES, LRU_WIDTH)
    b = b_ref[...].reshape(groups, V7X_SUBLANES, LRU_WIDTH)
    pos = lax.broadcasted_iota(jnp.int32, (1, V7X_SUBLANES, 1), 1)
    for s in (1, 2, 4):
        shift = (V7X_SUBLANES - s) if reverse else s
        a_n = pltpu.roll(a, shift, axis=1)
        b_n = pltpu.roll(b, shift, axis=1)
        ok = (pos < V7X_SUBLANES - s) if reverse else (pos >= s)
        b = jnp.where(ok, a * b_n + b, b)
        a = jnp.where(ok, a * a_n, a)
    a_ref[...] = a.reshape(n, LRU_WIDTH)
    b_ref[...] = b.reshape(n, LRU_WIDTH)

    @pl.when(is_ctx)
    def _():
        carry_ref[...] = jnp.zeros_like(carry_ref)

    edge = 0 if reverse else V7X_SUBLANES - 1

    def group_step(gi, carry):
        g = (groups - 1 - gi) if reverse else gi
        row = pl.multiple_of(g * V7X_SUBLANES, V7X_SUBLANES)
        hg = a_ref[pl.ds(row, V7X_SUBLANES), :] * carry + b_ref[pl.ds(row, V7X_SUBLANES), :]
        if reverse:
            h_ref[pl.ds(row, V7X_SUBLANES), :] = hg
        else:
            h_ref[pl.ds(row, V7X_SUBLANES), :] = ((hg + hb_ref[pl.ds(row, V7X_SUBLANES), :])
                                                  * gy_ref[pl.ds(row, V7X_SUBLANES), :])
        return hg[edge:edge + 1, :]

    carry_ref[...] = lax.fori_loop(0, groups, group_step, carry_ref[...], unroll=True)


def _lru_pass(x_in, xc_in, w_gate, b_gate, lam, batch, seq, reverse, conv=None, h_bwd=None, gy=None):
    n = LRU_CHUNK
    n_chunks = seq // n
    per_halo = n // HALO

    def chunk_of(c):
        return jnp.where(c == 0, (n_chunks - 1) if reverse else 0, (n_chunks - c) if reverse else (c - 1))

    cur = lambda b, c: (b * n_chunks + chunk_of(c), 0)
    prev = lambda b, c: (jnp.maximum((b * n_chunks + chunk_of(c)) * per_halo - 1, 0), 0)
    nxt = lambda b, c: (jnp.minimum((b * n_chunks + chunk_of(c) + 1) * per_halo, batch * seq // HALO - 1), 0)
    const2 = lambda b, c: (0, 0)
    const3 = lambda b, c: (0, 0, 0)
    big = pl.BlockSpec((n, LRU_WIDTH), cur)
    ctx_blk = pl.BlockSpec((n, LRU_WIDTH), lambda b, c: (b, 0))
    gates = [_resident((LRU_BLOCKS, LRU_BLOCK, 2 * LRU_BLOCK), const3),
             _resident((2, LRU_WIDTH), const2),
             _resident((1, LRU_WIDTH), const2)]
    wide = jax.ShapeDtypeStruct((batch * seq, LRU_WIDTH), F32)
    scan_scratch = [pltpu.VMEM((n, LRU_WIDTH), F32), pltpu.VMEM((n, LRU_WIDTH), F32), pltpu.VMEM((1, LRU_WIDTH), F32)]
    if reverse:
        conv_w, conv_b = conv
        in_specs = [big, ctx_blk, pl.BlockSpec((HALO, LRU_WIDTH), prev), pl.BlockSpec((HALO, LRU_WIDTH), nxt),
                    _resident((CONV_WIDTH, LRU_WIDTH), const2), _resident((1, LRU_WIDTH), const2)] + gates
        args = [x_in, xc_in, x_in, x_in, conv_w, conv_b, w_gate, b_gate, lam]
        out_specs = [big, big, ctx_blk]
        out_shape = [wide, wide, jax.ShapeDtypeStruct(xc_in.shape, F32)]
        scratch = scan_scratch
    else:
        in_specs = [big, ctx_blk] + gates + [big, big]
        args = [x_in, xc_in, w_gate, b_gate, lam, h_bwd, gy]
        out_specs = big
        out_shape = wide
        scratch = scan_scratch
    return pl.pallas_call(
        functools.partial(_lru_kernel, reverse=reverse, n_chunks=n_chunks),
        grid=(batch, n_chunks + 1),
        in_specs=in_specs,
        out_specs=out_specs,
        out_shape=out_shape,
        scratch_shapes=scratch,
        compiler_params=_cparams(("parallel", "arbitrary"), 40),
        name="lru_bwd" if reverse else "lru_fwd",
    )(*args)


def _merge_kernel(oa_ref, ob_ref, sga_ref, sgb_ref, x_ref, mod_ref, woa_ref, wol_ref, wout_ref,
                  n2_ref, wq_ref, keys_ref, x1_ref, h2_ref, st_ref):
    ma = jnp.dot(oa_ref[...], woa_ref[...], preferred_element_type=F32)
    mb = jnp.dot(ob_ref[...].astype(BF16), wol_ref[...], preferred_element_type=F32)
    merged = sga_ref[...] * ma + sgb_ref[...] * mb
    gate1 = mod_ref[:, 2 * D_MODEL:3 * D_MODEL]
    x1 = x_ref[...] + gate1 * jnp.dot(merged.astype(BF16), wout_ref[...], preferred_element_type=F32)
    x1_ref[...] = x1
    ms = jnp.mean(x1 * x1, axis=-1, keepdims=True)
    h = x1 * lax.rsqrt(ms + EPS) * n2_ref[...]
    shift2 = mod_ref[:, 3 * D_MODEL:4 * D_MODEL]
    scale2 = mod_ref[:, 4 * D_MODEL:5 * D_MODEL]
    h2 = (h * (1.0 + scale2) + shift2).astype(BF16)
    h2_ref[...] = h2
    qp = jnp.dot(h2, wq_ref[...], preferred_element_type=F32).astype(BF16)
    for hp in range(2 * PEER_HEADS):
        st_ref[hp] = lax.dot_general(keys_ref[hp], qp[:, hp * PEER_HALF:(hp + 1) * PEER_HALF], _NT,
                                     preferred_element_type=F32)


def _merge(o_a, o_b, sga, sgb, x2, mod3, w_oa, w_ol, w_out, n2, w_q, keys, rows_per_batch):
    t = x2.shape[0]
    tm = 512
    per = rows_per_batch // tm
    row = lambda i: (i, 0)
    const = lambda i: (0, 0)
    wide = pl.BlockSpec((tm, D_MODEL), row)
    nq = PEER_HEADS * PEER_QDIM
    return pl.pallas_call(
        _merge_kernel,
        grid=(t // tm,),
        in_specs=[pl.BlockSpec((tm, NA_WIDTH), row), wide, wide, wide, wide,
                  pl.BlockSpec((None, 1, 6 * D_MODEL), lambda i: (i // per, 0, 0)),
                  _resident((NA_WIDTH, D_MODEL), const),
                  _resident((LRU_WIDTH, D_MODEL), const),
                  _resident((D_MODEL, D_MODEL), const),
                  _resident((1, D_MODEL), const),
                  _resident((D_MODEL, nq), const),
                  _resident((2 * PEER_HEADS, PEER_NKEYS, PEER_HALF), lambda i: (0, 0, 0))],
        out_specs=[wide, wide, pl.BlockSpec((2 * PEER_HEADS, PEER_NKEYS, tm), lambda i: (0, 0, i))],
        out_shape=[jax.ShapeDtypeStruct((t, D_MODEL), F32),
                   jax.ShapeDtypeStruct((t, D_MODEL), BF16),
                   jax.ShapeDtypeStruct((2 * PEER_HEADS, PEER_NKEYS, t), F32)],
        compiler_params=_cparams(("parallel",), 56),
        name="merge",
    )(o_a, o_b, sga, sgb, x2, mod3, w_oa, w_ol, w_out, n2, w_q, keys)


def _top_ranked(x, k, with_rank=True):
    rank = jnp.full(x.shape, float(k), F32) if with_rank else None
    vals = []
    for b in range(k):
        m = jnp.max(x, axis=0, keepdims=True)
        hit = x >= m
        if with_rank:
            rank = jnp.where(hit, float(b), rank)
        x = jnp.where(hit, NEG_INF, x)
        vals.append(m)
    return vals, rank


def _stack_rows(rows):
    cols = rows[0].shape[1]
    slot = lax.broadcasted_iota(jnp.int32, (len(rows), cols), 0)
    out = jnp.zeros((len(rows), cols), F32)
    for b, row in enumerate(rows):
        out = jnp.where(slot == b, row, out)
    return out


_PAIR_LIMIT = [PEER_TOPK // (a + 1) for a in range(PEER_TOPK)]
_PAIR_HALF = PEER_TOPK // 2


def _select_kernel(st_ref, c1_ref, p1_ref, r2_ref, p2_ref):
    k = PEER_TOPK
    cols = st_ref.shape[2]

    slot = lax.broadcasted_iota(jnp.int32, (_PAIR_HALF, cols), 0)
    for h in range(PEER_HEADS):
        s1 = st_ref[2 * h]
        s2 = st_ref[2 * h + 1]
        v1, _ = _top_ranked(s1, k, with_rank=False)
        v2, rank2 = _top_ranked(s2, k)
        v2_all = _stack_rows(v2)
        groups = [v1[0] + v2_all]
        for a in range(1, _PAIR_HALF):
            groups.append(jnp.where(slot < _PAIR_LIMIT[a], v1[a] + v2_all[0:_PAIR_HALF], NEG_INF))
        groups.append(_stack_rows(v1[_PAIR_HALF:]) + v2[0])
        cand0 = jnp.concatenate(groups, axis=0)
        cand = cand0
        tau = jnp.full_like(v1[0], NEG_INF)
        seen = jnp.zeros_like(v1[0])
        for _ in range(k):
            m = jnp.max(cand, axis=0, keepdims=True)
            hit = cand >= m
            tau = jnp.where(seen < k, m, tau)
            seen = seen + jnp.sum(hit.astype(F32), axis=0, keepdims=True)
            cand = jnp.where(hit, NEG_INF, cand)
        picked = cand0 >= tau
        z = jnp.sum(jnp.where(picked, jnp.exp(cand0 - (v1[0] + v2[0])), 0.0), axis=0, keepdims=True)
        pf = picked.astype(F32)
        counts = [jnp.sum(pf[0:k], axis=0, keepdims=True)]
        for a in range(1, _PAIR_HALF):
            lo = k + (a - 1) * _PAIR_HALF
            counts.append(jnp.sum(pf[lo:lo + _PAIR_HALF], axis=0, keepdims=True))
        lo = k + (_PAIR_HALF - 1) * _PAIR_HALF
        counts += [pf[lo + a:lo + a + 1] for a in range(k - _PAIR_HALF)]
        c1 = jnp.zeros_like(s1)
        for a in range(k):
            c1 = jnp.where(s1 == v1[a], counts[a], c1)
        c1_ref[h] = c1
        p1_ref[h] = jnp.exp(s1 - v1[0])
        r2_ref[h * PEER_NKEYS:(h + 1) * PEER_NKEYS, :] = rank2.astype(BF16)
        p2_ref[h * PEER_NKEYS:(h + 1) * PEER_NKEYS, :] = (jnp.exp(s2 - v2[0]) / z).astype(BF16)


def _select(st):
    t = st.shape[2]
    tt = 256
    blk = pl.BlockSpec((PEER_HEADS, PEER_NKEYS, tt), lambda i: (0, 0, i))
    flat = pl.BlockSpec((PEER_HEADS * PEER_NKEYS, tt), lambda i: (0, i))
    full = jax.ShapeDtypeStruct((PEER_HEADS, PEER_NKEYS, t), F32)
    full_flat = jax.ShapeDtypeStruct((PEER_HEADS * PEER_NKEYS, t), BF16)
    return pl.pallas_call(
        _select_kernel,
        grid=(t // tt,),
        in_specs=[pl.BlockSpec((2 * PEER_HEADS, PEER_NKEYS, tt), lambda i: (0, 0, i))],
        out_specs=[blk, blk, flat, flat],
        out_shape=[full, full, full_flat, full_flat],
        compiler_params=_cparams(("parallel",), 40),
        name="select",
    )(st)


EXP_TOK = 512
EXP_BLK = 2048
EXP_I = EXP_BLK // PEER_NKEYS
GATE_ROWS = V7X_SUBLANES
GATE_LANES = 2 * V7X_LANES


def _gate_times_act(at_ref, m_ref, c1_ref, p1_ref, r2_ref, p2_ref):
    jt, jl = GATE_ROWS, GATE_LANES
    for lt in range(EXP_TOK // jl):
        lanes = slice(lt * jl, (lt + 1) * jl)
        c1_rows = [c1_ref[h, :, lanes] for h in range(PEER_HEADS)]
        p1_rows = [p1_ref[h, :, lanes] for h in range(PEER_HEADS)]
        for il in range(EXP_I):
            w = [None] * (PEER_NKEYS // jt)
            for h in range(PEER_HEADS):
                c1 = jnp.broadcast_to(c1_rows[h][il:il + 1, :], (jt, jl)).astype(BF16)
                p1 = jnp.broadcast_to(p1_rows[h][il:il + 1, :], (jt, jl)).astype(BF16)
                for jb in range(PEER_NKEYS // jt):
                    js = slice(h * PEER_NKEYS + jb * jt, h * PEER_NKEYS + (jb + 1) * jt)
                    term = jnp.where(r2_ref[js, lanes] < c1, p2_ref[js, lanes] * p1, 0)
                    w[jb] = term if h == 0 else w[jb] + term
            row0 = il * PEER_NKEYS
            m_ref[row0:row0 + PEER_NKEYS, lanes] = _gelu_bf16(at_ref[row0:row0 + PEER_NKEYS, lanes])
            for jb in range(PEER_NKEYS // jt):
                ms = slice(row0 + jb * jt, row0 + (jb + 1) * jt)
                m_ref[ms, lanes] = w[jb] * m_ref[ms, lanes]


def _experts_kernel(h2_ref, u_ref, v_ref, c1_ref, p1_ref, r2_ref, p2_ref, x1_ref, mod_ref,
                    o_ref, at_ref, m_ref, yt_ref):
    eb = pl.program_id(1)

    @pl.when(eb == 0)
    def _():
        yt_ref[...] = jnp.zeros_like(yt_ref)

    at_ref[...] = lax.dot_general(u_ref[...], h2_ref[...], _NT, preferred_element_type=F32)
    _gate_times_act(at_ref, m_ref, c1_ref, p1_ref, r2_ref, p2_ref)
    yt_ref[...] += lax.dot_general(v_ref[...], m_ref[...], (((0,), (0,)), ((), ())), preferred_element_type=F32)

    @pl.when(eb == pl.num_programs(1) - 1)
    def _():
        gate2 = mod_ref[:, 5 * D_MODEL:6 * D_MODEL]
        o_ref[...] = x1_ref[...] + gate2 * yt_ref[...].T


def _experts(h2, u_b, v_b, c1, p1, r2, p2, x1, mod3, rows_per_batch):
    t = h2.shape[0]
    per = rows_per_batch // EXP_TOK
    tok = lambda i, e: (i, 0)
    rows = pl.BlockSpec((PEER_HEADS, EXP_I, EXP_TOK), lambda i, e: (0, e, i))
    sel_flat = pl.BlockSpec((PEER_HEADS * PEER_NKEYS, EXP_TOK), lambda i, e: (0, i))
    return pl.pallas_call(
        _experts_kernel,
        grid=(t // EXP_TOK, PEER_EXPERTS // EXP_BLK),
        in_specs=[pl.BlockSpec((EXP_TOK, D_MODEL), tok),
                  pl.BlockSpec((EXP_BLK, D_MODEL), lambda i, e: (e, 0)),
                  pl.BlockSpec((EXP_BLK, D_MODEL), lambda i, e: (e, 0)),
                  rows, rows, sel_flat, sel_flat,
                  pl.BlockSpec((EXP_TOK, D_MODEL), tok),
                  pl.BlockSpec((None, 1, 6 * D_MODEL), lambda i, e: (i // per, 0, 0))],
        out_specs=pl.BlockSpec((EXP_TOK, D_MODEL), tok),
        out_shape=jax.ShapeDtypeStruct((t, D_MODEL), F32),
        scratch_shapes=[pltpu.VMEM((EXP_BLK, EXP_TOK), F32),
                        pltpu.VMEM((EXP_BLK, EXP_TOK), BF16),
                        pltpu.VMEM((D_MODEL, EXP_TOK), F32)],
        compiler_params=_cparams(("parallel", "arbitrary"), 58),
        name="experts",
    )(h2, u_b, v_b, c1, p1, r2, p2, x1, mod3)


def _layer(x, c, ctx, c_ctx, w_ada, b_ada, norm1_w, norm2_w, w_in, q_norm_w, k_norm_w, na_rpb,
           conv_w, conv_b, lru_w_a, lru_b_a, lru_w_x, lru_b_x, lru_lambda, w_o_attn, w_o_lru,
           w_out, peer_w_q, peer_keys, peer_u, peer_v):
    batch, seq, d = x.shape
    n_ctx = ctx.shape[1]
    assert d == D_MODEL and n_ctx == CTX_LEN and seq % (ATT_QROWS * GRID_W) == 0 and batch < MOD_ROWS
    x2 = x.reshape(batch * seq, d)
    ctx2 = ctx.reshape(batch * n_ctx, d)

    c_rows = jnp.zeros((MOD_ROWS, d), F32).at[:batch].set(c).at[batch].set(c_ctx)
    mod3 = _mod(c_rows, w_ada, b_ada).reshape(MOD_ROWS, 1, 6 * d)

    w_in_b = w_in.astype(BF16)
    n1 = norm1_w.reshape(1, d)
    qn_row = (jnp.tile(q_norm_w, NA_HEADS) * (NA_HEAD_DIM ** -0.5)).reshape(1, NA_WIDTH)
    kn_row = jnp.tile(k_norm_w, NA_HEADS).reshape(1, NA_WIDTH)
    qkv, xr, gy, sga, sgb = _inproj(x2, mod3, n1, w_in_b, qn_row, kn_row, seq)
    kv_c, xr_c = _inproj_ctx(ctx2, mod3, n1, w_in_b, kn_row, batch)

    o_a = _attention(qkv, kv_c, _attn_bias_table(na_rpb), batch, seq)

    def gate_params(direction):
        w_gate = jnp.concatenate([lru_w_a[direction], lru_w_x[direction]], axis=-1).astype(BF16)
        b_gate = jnp.stack([lru_b_a[direction], lru_b_x[direction]])
        return w_gate, b_gate, lru_lambda[direction].reshape(1, LRU_WIDTH)

    h_bwd, u, u_c = _lru_pass(xr, xr_c, *gate_params(1), batch, seq, True,
                              conv=(conv_w, conv_b.reshape(1, LRU_WIDTH)))
    o_b = _lru_pass(u, u_c, *gate_params(0), batch, seq, False, h_bwd=h_bwd, gy=gy)

    keys = peer_keys.reshape(2 * PEER_HEADS, PEER_NKEYS, PEER_HALF).astype(BF16)
    x1, h2, st = _merge(o_a, o_b, sga, sgb, x2, mod3, w_o_attn.astype(BF16), w_o_lru.astype(BF16),
                        w_out.astype(BF16), norm2_w.reshape(1, d), peer_w_q.astype(BF16), keys, seq)
    c1, p1, r2, p2 = _select(st)
    out = _experts(h2, peer_u.astype(BF16), peer_v.astype(BF16), c1, p1, r2, p2, x1, mod3, seq)
    return out.reshape(batch, seq, d)


def kernel(x, c, ctx, c_ctx, w_ada, b_ada, norm1_w, norm2_w, w_in, q_norm_w, k_norm_w, na_rpb, conv_w, conv_b, lru_w_a, lru_b_a, lru_w_x, lru_b_x, lru_lambda, w_o_attn, w_o_lru, w_out, peer_w_q, peer_keys, peer_u, peer_v):
    depth = w_ada.shape[0]
    for layer in range(depth):
        x = _layer(x, c, ctx, c_ctx, w_ada[layer], b_ada[layer], norm1_w[layer], norm2_w[layer],
                   w_in[layer], q_norm_w[layer], k_norm_w[layer], na_rpb[layer], conv_w[layer],
                   conv_b[layer], lru_w_a[layer], lru_b_a[layer], lru_w_x[layer], lru_b_x[layer],
                   lru_lambda[layer], w_o_attn[layer], w_o_lru[layer], w_out[layer], peer_w_q[layer],
                   peer_keys[layer], peer_u[layer], peer_v[layer])
    return x
```

```python
import functools

import numpy as np
import jax
import jax.numpy as jnp
from jax import lax
from jax.experimental import pallas as pl
from jax.experimental.pallas import tpu as pltpu

F32 = jnp.float32
BF16 = jnp.bfloat16

D_MODEL = 1024
GRID_W = 64
CTX_LEN = 256
NA_HEADS = 8
NA_HEAD_DIM = 64
NA_WIDTH = NA_HEADS * NA_HEAD_DIM
NA_WIN_ROWS = 8
NA_WIN_COLS = 16
LRU_WIDTH = D_MODEL
LRU_BLOCKS = 8
LRU_BLOCK = LRU_WIDTH // LRU_BLOCKS
CONV_WIDTH = 4
CONV_LEFT = 2
LRU_C = 8.0
PEER_HEADS = 8
PEER_NKEYS = 128
PEER_EXPERTS = PEER_NKEYS * PEER_NKEYS
PEER_QDIM = 256
PEER_HALF = PEER_QDIM // 2
PEER_TOPK = 16
EPS = 1e-6
NEG_INF = -1e30

Q0 = 0
K0 = Q0 + NA_WIDTH
V0 = K0 + NA_WIDTH
XR0 = V0 + NA_WIDTH
YR0 = XR0 + LRU_WIDTH
GA0 = YR0 + LRU_WIDTH
GB0 = GA0 + D_MODEL
IN_COLS = GB0 + D_MODEL

V7X_LANES = 128
V7X_SUBLANES = 8

MOD_ROWS = 8
ATT_QROWS = 16
ATT_SUB = 4
ATT_KROWS = 12
LRU_CHUNK = 256
HALO = V7X_SUBLANES

_NT = (((1,), (1,)), ((), ()))


def _cparams(sem, vmem_mb, flags=None):
    return pltpu.CompilerParams(dimension_semantics=sem, vmem_limit_bytes=vmem_mb * 1024 * 1024, flags=flags)


def _resident(shape, index_map):
    return pl.BlockSpec(shape, index_map, pipeline_mode=pl.Buffered(1))


def _gelu(x):
    return jax.nn.gelu(x)


def _sigmoid(x):
    return jax.nn.sigmoid(x)


_GELU_A = -2.0 * float(np.sqrt(2.0 / np.pi) * np.log2(np.e))
_GELU_B = _GELU_A * 0.044715


def _gelu_bf16(x):
    xb = x.astype(BF16)
    z = (xb * xb * _GELU_B + _GELU_A) * xb
    return xb / (1.0 + jnp.exp2(z))


def _mod_kernel(c_ref, w_ref, b_ref, o_ref):
    cs = c_ref[...]
    s = cs * _sigmoid(cs)
    o_ref[...] = jnp.dot(s, w_ref[...], preferred_element_type=F32,
                         precision=lax.Precision.HIGHEST) + b_ref[...]


def _mod(c_rows, w_ada, b_ada):
    n = w_ada.shape[1]
    tn = 1536
    return pl.pallas_call(
        _mod_kernel,
        grid=(n // tn,),
        in_specs=[pl.BlockSpec((MOD_ROWS, D_MODEL), lambda j: (0, 0)),
                  pl.BlockSpec((D_MODEL, tn), lambda j: (0, j)),
                  pl.BlockSpec((1, tn), lambda j: (0, j))],
        out_specs=pl.BlockSpec((MOD_ROWS, tn), lambda j: (0, j)),
        out_shape=jax.ShapeDtypeStruct((MOD_ROWS, n), F32),
        compiler_params=_cparams(("arbitrary",), 40),
        name="mod",
    )(c_rows, w_ada, b_ada.reshape(1, n))


def _normed_input(x_ref, mod_ref, n1_ref):
    x = x_ref[...]
    ms = jnp.mean(x * x, axis=-1, keepdims=True)
    h = x * lax.rsqrt(ms + EPS) * n1_ref[...]
    shift = mod_ref[:, 0:D_MODEL]
    scale = mod_ref[:, D_MODEL:2 * D_MODEL]
    return (h * (1.0 + scale) + shift).astype(BF16)


def _head_norm(z, g_ref, w_row):
    zz = z * z
    hi = zz.astype(BF16)
    lo = (zz - hi.astype(F32)).astype(BF16)
    ms = (jnp.dot(hi, g_ref[...], preferred_element_type=F32)
          + jnp.dot(lo, g_ref[...], preferred_element_type=F32))
    return z * lax.rsqrt(ms + EPS) * w_row


def _inproj_kernel(x_ref, mod_ref, n1_ref, w_ref, g_ref, qn_ref, kn_ref,
                   qkv_ref, xr_ref, gy_ref, sga_ref, sgb_ref):
    hb = _normed_input(x_ref, mod_ref, n1_ref)
    seg = lambda lo, hi: jnp.dot(hb, w_ref[:, lo:hi], preferred_element_type=F32)
    qkv_ref[:, Q0:K0] = _head_norm(seg(Q0, K0), g_ref, qn_ref[...]).astype(BF16)
    qkv_ref[:, K0:V0] = _head_norm(seg(K0, V0), g_ref, kn_ref[...]).astype(BF16)
    qkv_ref[:, V0:XR0] = seg(V0, XR0).astype(BF16)
    xr_ref[...] = seg(XR0, YR0)
    gy_ref[...] = _gelu(seg(YR0, GA0))
    sga_ref[...] = _sigmoid(seg(GA0, GB0))
    sgb_ref[...] = _sigmoid(seg(GB0, IN_COLS))


def _inproj_ctx_kernel(x_ref, mod_ref, n1_ref, w_ref, g_ref, kn_ref, kv_ref, xr_ref):
    hb = _normed_input(x_ref, mod_ref, n1_ref)
    seg = lambda lo, hi: jnp.dot(hb, w_ref[:, lo:hi], preferred_element_type=F32)
    kv_ref[:, 0:NA_WIDTH] = _head_norm(seg(K0, V0), g_ref, kn_ref[...]).astype(BF16)
    kv_ref[:, NA_WIDTH:2 * NA_WIDTH] = seg(V0, XR0).astype(BF16)
    xr_ref[...] = seg(XR0, YR0)


def _head_mean_matrix():
    head = np.arange(NA_WIDTH) // NA_HEAD_DIM
    return jnp.asarray((head[:, None] == head[None, :]).astype(np.float32) / NA_HEAD_DIM, dtype=BF16)


def _inproj(x2, mod3, n1, w_in_b, qn_row, kn_row, rows_per_batch):
    t = x2.shape[0]
    tm = 512
    per = rows_per_batch // tm
    row = lambda i: (i, 0)
    const = lambda i: (0, 0)
    wide = pl.BlockSpec((tm, D_MODEL), row)
    return pl.pallas_call(
        _inproj_kernel,
        grid=(t // tm,),
        in_specs=[wide,
                  pl.BlockSpec((None, 1, 6 * D_MODEL), lambda i: (i // per, 0, 0)),
                  _resident((1, D_MODEL), const),
                  _resident((D_MODEL, IN_COLS), const),
                  _resident((NA_WIDTH, NA_WIDTH), const),
                  _resident((1, NA_WIDTH), const),
                  _resident((1, NA_WIDTH), const)],
        out_specs=[pl.BlockSpec((tm, 3 * NA_WIDTH), row), wide, wide, wide, wide],
        out_shape=[jax.ShapeDtypeStruct((t, 3 * NA_WIDTH), BF16)] + [jax.ShapeDtypeStruct((t, D_MODEL), F32)] * 4,
        compiler_params=_cparams(("parallel",), 56),
        name="inproj",
    )(x2, mod3, n1, w_in_b, _head_mean_matrix(), qn_row, kn_row)


def _inproj_ctx(ctx2, mod3, n1, w_in_b, kn_row, ctx_mod_row):
    t = ctx2.shape[0]
    tm = 512
    row = lambda i: (i, 0)
    const = lambda i: (0, 0)
    wide = pl.BlockSpec((tm, D_MODEL), row)
    return pl.pallas_call(
        _inproj_ctx_kernel,
        grid=(t // tm,),
        in_specs=[wide,
                  pl.BlockSpec((None, 1, 6 * D_MODEL), lambda i: (ctx_mod_row, 0, 0)),
                  _resident((1, D_MODEL), const),
                  _resident((D_MODEL, IN_COLS), const),
                  _resident((NA_WIDTH, NA_WIDTH), const),
                  _resident((1, NA_WIDTH), const)],
        out_specs=[wide, wide],
        out_shape=[jax.ShapeDtypeStruct((t, 2 * NA_WIDTH), BF16), jax.ShapeDtypeStruct((t, D_MODEL), F32)],
        compiler_params=_cparams(("parallel",), 56),
        name="inproj_ctx",
    )(ctx2, mod3, n1, w_in_b, _head_mean_matrix(), kn_row)


def _attn_bias_table(rpb):
    wr, wc, w = NA_WIN_ROWS, NA_WIN_COLS, GRID_W
    col = np.arange(w)
    col0 = np.clip(col - wc // 2, 0, w - wc)
    col_mask = (col[None, :] >= col0[:, None]) & (col[None, :] < col0[:, None] + wc)
    dc = np.clip(col[None, :] - col[:, None], -(wc - 1), wc - 1) + wc - 1
    onehot = (dc[None] == np.arange(2 * wc - 1)[:, None, None]).astype(np.float32)
    tab = jnp.einsum('hdc,cqk->hdqk', rpb, onehot, precision=lax.Precision.HIGHEST)
    tab = jnp.where(col_mask[None, None], tab, NEG_INF)
    none = jnp.full((NA_HEADS, 1, w, w), NEG_INF, F32)
    return jnp.concatenate([jnp.concatenate([none, tab], axis=1),
                            jnp.concatenate([tab, none], axis=1)], axis=-1)


def _att_key_start(first_qrow, rows):
    return np.clip(first_qrow - NA_WIN_ROWS // 2, 0, rows - ATT_KROWS)


def _fill_bias(tab_ref, bias_ref, rb, rows):
    wr, w = NA_WIN_ROWS, GRID_W
    lane = lax.broadcasted_iota(jnp.int32, (w, 2 * w), 1)
    for hh in range(V7X_LANES // NA_HEAD_DIM):
        for sub in range(ATT_QROWS // ATT_SUB):
            ks = int(_att_key_start(rb * ATT_QROWS + sub * ATT_SUB, rows))
            for i in range(ATT_SUB):
                r = rb * ATT_QROWS + sub * ATT_SUB + i
                row0 = int(np.clip(r - wr // 2, 0, rows - wr))
                assert ks <= row0 and row0 + wr <= ks + ATT_KROWS
                for w2 in range(ATT_KROWS // 2):
                    kr = ks + 2 * w2
                    left = row0 <= kr < row0 + wr
                    right = row0 <= kr + 1 < row0 + wr
                    if left or right:
                        blk = tab_ref[hh, kr - r + wr]
                        if not right:
                            blk = jnp.where(lane < w, blk, NEG_INF)
                        elif not left:
                            blk = jnp.where(lane >= w, blk, NEG_INF)
                    else:
                        blk = jnp.full((w, 2 * w), NEG_INF, F32)
                    bias_ref[hh, sub, i * w:(i + 1) * w, w2 * 2 * w:(w2 + 1) * 2 * w] = blk


def _attn_kernel(q_ref, k_ref, v_ref, kc_ref, vc_ref, tab_ref, o_ref, bias_ref, *, rows):
    rb = pl.program_id(1)
    n_rb = rows // ATT_QROWS

    for rb_static in (0, 1, n_rb - 1):
        @pl.when(jnp.logical_and(pl.program_id(2) == 0, rb == rb_static))
        def _():
            _fill_bias(tab_ref, bias_ref, rb_static, rows)

    nq, nk = ATT_SUB * GRID_W, ATT_KROWS * GRID_W
    kc = kc_ref[...]
    vc = vc_ref[...]
    lane = lax.broadcasted_iota(jnp.int32, (1, V7X_LANES), 1)
    for sub in range(ATT_QROWS // ATT_SUB):
        first = rb * ATT_QROWS + sub * ATT_SUB
        ks = jnp.clip(first - NA_WIN_ROWS // 2, 0, rows - ATT_KROWS) * GRID_W
        ks = pl.multiple_of(ks, GRID_W)
        q = q_ref[sub * nq:(sub + 1) * nq, :]
        kw = k_ref[pl.ds(ks, nk), :]
        vw = v_ref[pl.ds(ks, nk), :]
        acc = jnp.zeros(q.shape, F32)
        for hh in range(V7X_LANES // NA_HEAD_DIM):
            sel = (lane // NA_HEAD_DIM == hh).astype(BF16)
            qh = q * sel
            s = lax.dot_general(qh, kw, _NT, preferred_element_type=F32) + bias_ref[hh, sub]
            sc = lax.dot_general(qh, kc, _NT, preferred_element_type=F32)
            mx = jnp.maximum(jnp.max(s, axis=-1, keepdims=True), jnp.max(sc, axis=-1, keepdims=True))
            p = jnp.exp(s - mx)
            pc = jnp.exp(sc - mx)
            den = jnp.sum(p, axis=-1, keepdims=True) + jnp.sum(pc, axis=-1, keepdims=True)
            o = (jnp.dot(p.astype(BF16), vw * sel, preferred_element_type=F32)
                 + jnp.dot(pc.astype(BF16), vc * sel, preferred_element_type=F32))
            acc = acc + o / den
        o_ref[sub * nq:(sub + 1) * nq, :] = acc.astype(BF16)


def _attention(qkv, kv_c, tab, batch, seq):
    rows = seq // GRID_W
    n_rb = rows // ATT_QROWS
    tq = ATT_QROWS * GRID_W
    pairs = NA_WIDTH // V7X_LANES
    kcol, vcol = K0 // V7X_LANES, V0 // V7X_LANES
    assert n_rb >= 3
    heads_per_step = V7X_LANES // NA_HEAD_DIM
    return pl.pallas_call(
        functools.partial(_attn_kernel, rows=rows),
        grid=(pairs, n_rb, batch),
        in_specs=[pl.BlockSpec((tq, V7X_LANES), lambda hp, rb, b: (b * n_rb + rb, hp)),
                  pl.BlockSpec((seq, V7X_LANES), lambda hp, rb, b: (b, kcol + hp)),
                  pl.BlockSpec((seq, V7X_LANES), lambda hp, rb, b: (b, vcol + hp)),
                  pl.BlockSpec((CTX_LEN, V7X_LANES), lambda hp, rb, b: (b, hp)),
                  pl.BlockSpec((CTX_LEN, V7X_LANES), lambda hp, rb, b: (b, pairs + hp)),
                  pl.BlockSpec((heads_per_step, 2 * NA_WIN_ROWS, GRID_W, 2 * GRID_W),
                               lambda hp, rb, b: (hp, 0, 0, 0))],
        out_specs=pl.BlockSpec((tq, V7X_LANES), lambda hp, rb, b: (b * n_rb + rb, hp)),
        out_shape=jax.ShapeDtypeStruct((batch * seq, NA_WIDTH), BF16),
        scratch_shapes=[pltpu.VMEM((heads_per_step, ATT_QROWS // ATT_SUB, ATT_SUB * GRID_W, ATT_KROWS * GRID_W),
                                   F32)],
        compiler_params=_cparams(("arbitrary", "arbitrary", "arbitrary"), 48),
        name="attn",
    )(qkv, qkv, qkv, kv_c, kv_c, tab)


def _lru_kernel(*refs, reverse, n_chunks):
    if reverse:
        (x_ref, xc_ref, prev_ref, next_ref, cw_ref, cb_ref, wg_ref, bg_ref, lam_ref,
         h_ref, u_ref, uc_ref, a_ref, b_ref, carry_ref) = refs
    else:
        (u_ref, uc_ref, wg_ref, bg_ref, lam_ref, hb_ref, gy_ref, h_ref, a_ref, b_ref, carry_ref) = refs
    c = pl.program_id(1)
    is_ctx = c == 0
    n = LRU_CHUNK

    if reverse:
        chunk = n_chunks - c
        has_prev = jnp.logical_and(c > 0, chunk > 0)
        has_next = jnp.logical_and(c > 0, chunk < n_chunks - 1)
        g8 = n // HALO
        win = jnp.concatenate([jnp.where(has_prev, prev_ref[...], 0.0)[None],
                               jnp.where(is_ctx, xc_ref[...], x_ref[...]).reshape(g8, HALO, LRU_WIDTH),
                               jnp.where(has_next, next_ref[...], 0.0)[None]], axis=0)
        pos8 = lax.broadcasted_iota(jnp.int32, (1, HALO, 1), 1)
        u = cb_ref[...] + cw_ref[CONV_LEFT:CONV_LEFT + 1, :] * win[1:g8 + 1]
        for j in range(CONV_WIDTH):
            d = j - CONV_LEFT
            if d == 0:
                continue
            rot = pltpu.roll(win, (-d) % HALO, axis=1)
            if d < 0:
                tap = jnp.where(pos8 < -d, rot[0:g8], rot[1:g8 + 1])
            else:
                tap = jnp.where(pos8 >= HALO - d, rot[2:g8 + 2], rot[1:g8 + 1])
            u = u + cw_ref[j:j + 1, :] * tap
        u = u.reshape(n, LRU_WIDTH)
        u_ref[...] = u

        @pl.when(is_ctx)
        def _():
            uc_ref[...] = u
    else:
        u = jnp.where(is_ctx, uc_ref[...], u_ref[...])

    ub = u.astype(BF16)
    lam = lam_ref[...]
    softplus = jnp.maximum(-lam, 0.0) + jnp.log(1.0 + jnp.exp(-jnp.abs(lam)))
    for blk in range(LRU_BLOCKS):
        lo, hi = blk * LRU_BLOCK, (blk + 1) * LRU_BLOCK
        g = jnp.dot(ub[:, lo:hi], wg_ref[blk], preferred_element_type=F32)
        r = _sigmoid(g[:, 0:LRU_BLOCK] + bg_ref[0:1, lo:hi])
        i = _sigmoid(g[:, LRU_BLOCK:2 * LRU_BLOCK] + bg_ref[1:2, lo:hi])
        log_a = -LRU_C * r * softplus[:, lo:hi]
        a = jnp.exp(log_a)
        beta = jnp.sqrt(1.0 - a * a)
        a_ref[:, lo:hi] = a
        b_ref[:, lo:hi] = beta * (i * u[:, lo:hi])

    groups = n // V7X_SUBLANES
    a = a_ref[...].reshape(groups, V7X_SUBLANES, LRU_WIDTH)
    b = b_ref[...].reshape(groups, V7X_SUBLANES, LRU_WIDTH)
    pos = lax.broadcasted_iota(jnp.int32, (1, V7X_SUBLANES, 1), 1)
    for s in (1, 2, 4):
        shift = (V7X_SUBLANES - s) if reverse else s
        a_n = pltpu.roll(a, shift, axis=1)
        b_n = pltpu.roll(b, shift, axis=1)
        ok = (pos < V7X_SUBLANES - s) if reverse else (pos >= s)
        b = jnp.where(ok, a * b_n + b, b)
        a = jnp.where(ok, a * a_n, a)
    a_ref[...] = a.reshape(n, LRU_WIDTH)
    b_ref[...] = b.reshape(n, LRU_WIDTH)

    @pl.when(is_ctx)
    def _():
        carry_ref[...] = jnp.zeros_like(carry_ref)

    edge = 0 if reverse else V7X_SUBLANES - 1

    def group_step(gi, carry):
        g = (groups - 1 - gi) if reverse else gi
        row = pl.multiple_of(g * V7X_SUBLANES, V7X_SUBLANES)
        hg = a_ref[pl.ds(row, V7X_SUBLANES), :] * carry + b_ref[pl.ds(row, V7X_SUBLANES), :]
        if reverse:
            h_ref[pl.ds(row, V7X_SUBLANES), :] = hg
        else:
            h_ref[pl.ds(row, V7X_SUBLANES), :] = ((hg + hb_ref[pl.ds(row, V7X_SUBLANES), :])
                                                  * gy_ref[pl.ds(row, V7X_SUBLANES), :])
        return hg[edge:edge + 1, :]

    carry_ref[...] = lax.fori_loop(0, groups, group_step, carry_ref[...], unroll=True)


def _lru_pass(x_in, xc_in, w_gate, b_gate, lam, batch, seq, reverse, conv=None, h_bwd=None, gy=None):
    n = LRU_CHUNK
    n_chunks = seq // n
    per_halo = n // HALO

    def chunk_of(c):
        return jnp.where(c == 0, (n_chunks - 1) if reverse else 0, (n_chunks - c) if reverse else (c - 1))

    cur = lambda b, c: (b * n_chunks + chunk_of(c), 0)
    prev = lambda b, c: (jnp.maximum((b * n_chunks + chunk_of(c)) * per_halo - 1, 0), 0)
    nxt = lambda b, c: (jnp.minimum((b * n_chunks + chunk_of(c) + 1) * per_halo, batch * seq // HALO - 1), 0)
    const2 = lambda b, c: (0, 0)
    const3 = lambda b, c: (0, 0, 0)
    big = pl.BlockSpec((n, LRU_WIDTH), cur)
    ctx_blk = pl.BlockSpec((n, LRU_WIDTH), lambda b, c: (b, 0))
    gates = [_resident((LRU_BLOCKS, LRU_BLOCK, 2 * LRU_BLOCK), const3),
             _resident((2, LRU_WIDTH), const2),
             _resident((1, LRU_WIDTH), const2)]
    wide = jax.ShapeDtypeStruct((batch * seq, LRU_WIDTH), F32)
    scan_scratch = [pltpu.VMEM((n, LRU_WIDTH), F32), pltpu.VMEM((n, LRU_WIDTH), F32), pltpu.VMEM((1, LRU_WIDTH), F32)]
    if reverse:
        conv_w, conv_b = conv
        in_specs = [big, ctx_blk, pl.BlockSpec((HALO, LRU_WIDTH), prev), pl.BlockSpec((HALO, LRU_WIDTH), nxt),
                    _resident((CONV_WIDTH, LRU_WIDTH), const2), _resident((1, LRU_WIDTH), const2)] + gates
        args = [x_in, xc_in, x_in, x_in, conv_w, conv_b, w_gate, b_gate, lam]
        out_specs = [big, big, ctx_blk]
        out_shape = [wide, wide, jax.ShapeDtypeStruct(xc_in.shape, F32)]
        scratch = scan_scratch
    else:
        in_specs = [big, ctx_blk] + gates + [big, big]
        args = [x_in, xc_in, w_gate, b_gate, lam, h_bwd, gy]
        out_specs = big
        out_shape = wide
        scratch = scan_scratch
    return pl.pallas_call(
        functools.partial(_lru_kernel, reverse=reverse, n_chunks=n_chunks),
        grid=(batch, n_chunks + 1),
        in_specs=in_specs,
        out_specs=out_specs,
        out_shape=out_shape,
        scratch_shapes=scratch,
        compiler_params=_cparams(("parallel", "arbitrary"), 40),
        name="lru_bwd" if reverse else "lru_fwd",
    )(*args)


def _merge_kernel(oa_ref, ob_ref, sga_ref, sgb_ref, x_ref, mod_ref, woa_ref, wol_ref, wout_ref,
                  n2_ref, wq_ref, keys_ref, x1_ref, h2_ref, st_ref):
    ma = jnp.dot(oa_ref[...], woa_ref[...], preferred_element_type=F32)
    mb = jnp.dot(ob_ref[...].astype(BF16), wol_ref[...], preferred_element_type=F32)
    merged = sga_ref[...] * ma + sgb_ref[...] * mb
    gate1 = mod_ref[:, 2 * D_MODEL:3 * D_MODEL]
    x1 = x_ref[...] + gate1 * jnp.dot(merged.astype(BF16), wout_ref[...], preferred_element_type=F32)
    x1_ref[...] = x1
    ms = jnp.mean(x1 * x1, axis=-1, keepdims=True)
    h = x1 * lax.rsqrt(ms + EPS) * n2_ref[...]
    shift2 = mod_ref[:, 3 * D_MODEL:4 * D_MODEL]
    scale2 = mod_ref[:, 4 * D_MODEL:5 * D_MODEL]
    h2 = (h * (1.0 + scale2) + shift2).astype(BF16)
    h2_ref[...] = h2
    qp = jnp.dot(h2, wq_ref[...], preferred_element_type=F32).astype(BF16)
    for hp in range(2 * PEER_HEADS):
        st_ref[hp] = lax.dot_general(keys_ref[hp], qp[:, hp * PEER_HALF:(hp + 1) * PEER_HALF], _NT,
                                     preferred_element_type=F32)


def _merge(o_a, o_b, sga, sgb, x2, mod3, w_oa, w_ol, w_out, n2, w_q, keys, rows_per_batch):
    t = x2.shape[0]
    tm = 512
    per = rows_per_batch // tm
    row = lambda i: (i, 0)
    const = lambda i: (0, 0)
    wide = pl.BlockSpec((tm, D_MODEL), row)
    nq = PEER_HEADS * PEER_QDIM
    return pl.pallas_call(
        _merge_kernel,
        grid=(t // tm,),
        in_specs=[pl.BlockSpec((tm, NA_WIDTH), row), wide, wide, wide, wide,
                  pl.BlockSpec((None, 1, 6 * D_MODEL), lambda i: (i // per, 0, 0)),
                  _resident((NA_WIDTH, D_MODEL), const),
                  _resident((LRU_WIDTH, D_MODEL), const),
                  _resident((D_MODEL, D_MODEL), const),
                  _resident((1, D_MODEL), const),
                  _resident((D_MODEL, nq), const),
                  _resident((2 * PEER_HEADS, PEER_NKEYS, PEER_HALF), lambda i: (0, 0, 0))],
        out_specs=[wide, wide, pl.BlockSpec((2 * PEER_HEADS, PEER_NKEYS, tm), lambda i: (0, 0, i))],
        out_shape=[jax.ShapeDtypeStruct((t, D_MODEL), F32),
                   jax.ShapeDtypeStruct((t, D_MODEL), BF16),
                   jax.ShapeDtypeStruct((2 * PEER_HEADS, PEER_NKEYS, t), F32)],
        compiler_params=_cparams(("parallel",), 56),
        name="merge",
    )(o_a, o_b, sga, sgb, x2, mod3, w_oa, w_ol, w_out, n2, w_q, keys)


def _top_ranked(x, k, with_rank=True):
    rank = jnp.full(x.shape, float(k), F32) if with_rank else None
    vals = []
    for b in range(k):
        m = jnp.max(x, axis=0, keepdims=True)
        hit = x >= m
        if with_rank:
            rank = jnp.where(hit, float(b), rank)
        x = jnp.where(hit, NEG_INF, x)
        vals.append(m)
    return vals, rank


def _stack_rows(rows):
    cols = rows[0].shape[1]
    slot = lax.broadcasted_iota(jnp.int32, (len(rows), cols), 0)
    out = jnp.zeros((len(rows), cols), F32)
    for b, row in enumerate(rows):
        out = jnp.where(slot == b, row, out)
    return out


_PAIR_LIMIT = [PEER_TOPK // (a + 1) for a in range(PEER_TOPK)]
_PAIR_HALF = PEER_TOPK // 2


def _select_kernel(st_ref, c1_ref, p1_ref, r2_ref, p2_ref):
    k = PEER_TOPK
    cols = st_ref.shape[2]

    slot = lax.broadcasted_iota(jnp.int32, (_PAIR_HALF, cols), 0)
    for h in range(PEER_HEADS):
        s1 = st_ref[2 * h]
        s2 = st_ref[2 * h + 1]
        v1, _ = _top_ranked(s1, k, with_rank=False)
        v2, rank2 = _top_ranked(s2, k)
        v2_all = _stack_rows(v2)
        groups = [v1[0] + v2_all]
        for a in range(1, _PAIR_HALF):
            groups.append(jnp.where(slot < _PAIR_LIMIT[a], v1[a] + v2_all[0:_PAIR_HALF], NEG_INF))
        groups.append(_stack_rows(v1[_PAIR_HALF:]) + v2[0])
        cand0 = jnp.concatenate(groups, axis=0)
        cand = cand0
        tau = jnp.full_like(v1[0], NEG_INF)
        seen = jnp.zeros_like(v1[0])
        for _ in range(k):
            m = jnp.max(cand, axis=0, keepdims=True)
            hit = cand >= m
            tau = jnp.where(seen < k, m, tau)
            seen = seen + jnp.sum(hit.astype(F32), axis=0, keepdims=True)
            cand = jnp.where(hit, NEG_INF, cand)
        picked = cand0 >= tau
        z = jnp.sum(jnp.where(picked, jnp.exp(cand0 - (v1[0] + v2[0])), 0.0), axis=0, keepdims=True)
        pf = picked.astype(F32)
        counts = [jnp.sum(pf[0:k], axis=0, keepdims=True)]
        for a in range(1, _PAIR_HALF):
            lo = k + (a - 1) * _PAIR_HALF
            counts.append(jnp.sum(pf[lo:lo + _PAIR_HALF], axis=0, keepdims=True))
        lo = k + (_PAIR_HALF - 1) * _PAIR_HALF
        counts += [pf[lo + a:lo + a + 1] for a in range(k - _PAIR_HALF)]
        c1 = jnp.zeros_like(s1)
        for a in range(k):
            c1 = jnp.where(s1 == v1[a], counts[a], c1)
        c1_ref[h] = c1
        p1_ref[h] = jnp.exp(s1 - v1[0])
        r2_ref[h * PEER_NKEYS:(h + 1) * PEER_NKEYS, :] = rank2.astype(BF16)
        p2_ref[h * PEER_NKEYS:(h + 1) * PEER_NKEYS, :] = (jnp.exp(s2 - v2[0]) / z).astype(BF16)


def _select(st):
    t = st.shape[2]
    tt = 256
    blk = pl.BlockSpec((PEER_HEADS, PEER_NKEYS, tt), lambda i: (0, 0, i))
    flat = pl.BlockSpec((PEER_HEADS * PEER_NKEYS, tt), lambda i: (0, i))
    full = jax.ShapeDtypeStruct((PEER_HEADS, PEER_NKEYS, t), F32)
    full_flat = jax.ShapeDtypeStruct((PEER_HEADS * PEER_NKEYS, t), BF16)
    return pl.pallas_call(
        _select_kernel,
        grid=(t // tt,),
        in_specs=[pl.BlockSpec((2 * PEER_HEADS, PEER_NKEYS, tt), lambda i: (0, 0, i))],
        out_specs=[blk, blk, flat, flat],
        out_shape=[full, full, full_flat, full_flat],
        compiler_params=_cparams(("parallel",), 40),
        name="select",
    )(st)


EXP_TOK = 512
EXP_BLK = 2048
EXP_I = EXP_BLK // PEER_NKEYS
GATE_ROWS = V7X_SUBLANES
GATE_LANES = 2 * V7X_LANES


def _gate_times_act(at_ref, m_ref, c1_ref, p1_ref, r2_ref, p2_ref):
    jt, jl = GATE_ROWS, GATE_LANES
    for lt in range(EXP_TOK // jl):
        lanes = slice(lt * jl, (lt + 1) * jl)
        c1_rows = [c1_ref[h, :, lanes] for h in range(PEER_HEADS)]
        p1_rows = [p1_ref[h, :, lanes] for h in range(PEER_HEADS)]
        for il in range(EXP_I):
            w = [None] * (PEER_NKEYS // jt)
            for h in range(PEER_HEADS):
                c1 = jnp.broadcast_to(c1_rows[h][il:il + 1, :], (jt, jl)).astype(BF16)
                p1 = jnp.broadcast_to(p1_rows[h][il:il + 1, :], (jt, jl)).astype(BF16)
                for jb in range(PEER_NKEYS // jt):
                    js = slice(h * PEER_NKEYS + jb * jt, h * PEER_NKEYS + (jb + 1) * jt)
                    term = jnp.where(r2_ref[js, lanes] < c1, p2_ref[js, lanes] * p1, 0)
                    w[jb] = term if h == 0 else w[jb] + term
            row0 = il * PEER_NKEYS
            m_ref[row0:row0 + PEER_NKEYS, lanes] = _gelu_bf16(at_ref[row0:row0 + PEER_NKEYS, lanes])
            for jb in range(PEER_NKEYS // jt):
                ms = slice(row0 + jb * jt, row0 + (jb + 1) * jt)
                m_ref[ms, lanes] = w[jb] * m_ref[ms, lanes]


def _experts_kernel(h2_ref, u_ref, vt_ref, c1_ref, p1_ref, r2_ref, p2_ref, x1_ref, mod_ref,
                    o_ref, at_ref, m_ref, yt_ref):
    eb = pl.program_id(1)

    @pl.when(eb == 0)
    def _():
        yt_ref[...] = jnp.zeros_like(yt_ref)

    at_ref[...] = lax.dot_general(u_ref[...], h2_ref[...], _NT, preferred_element_type=F32)
    _gate_times_act(at_ref, m_ref, c1_ref, p1_ref, r2_ref, p2_ref)
    yt_ref[...] += jnp.dot(vt_ref[...], m_ref[...], preferred_element_type=F32)

    @pl.when(eb == pl.num_programs(1) - 1)
    def _():
        gate2 = mod_ref[:, 5 * D_MODEL:6 * D_MODEL]
        o_ref[...] = x1_ref[...] + gate2 * yt_ref[...].T


def _experts(h2, u_b, vt_b, c1, p1, r2, p2, x1, mod3, rows_per_batch):
    t = h2.shape[0]
    per = rows_per_batch // EXP_TOK
    tok = lambda i, e: (i, 0)
    rows = pl.BlockSpec((PEER_HEADS, EXP_I, EXP_TOK), lambda i, e: (0, e, i))
    sel_flat = pl.BlockSpec((PEER_HEADS * PEER_NKEYS, EXP_TOK), lambda i, e: (0, i))
    return pl.pallas_call(
        _experts_kernel,
        grid=(t // EXP_TOK, PEER_EXPERTS // EXP_BLK),
        in_specs=[pl.BlockSpec((EXP_TOK, D_MODEL), tok),
                  pl.BlockSpec((EXP_BLK, D_MODEL), lambda i, e: (e, 0)),
                  pl.BlockSpec((D_MODEL, EXP_BLK), lambda i, e: (0, e)),
                  rows, rows, sel_flat, sel_flat,
                  pl.BlockSpec((EXP_TOK, D_MODEL), tok),
                  pl.BlockSpec((None, 1, 6 * D_MODEL), lambda i, e: (i // per, 0, 0))],
        out_specs=pl.BlockSpec((EXP_TOK, D_MODEL), tok),
        out_shape=jax.ShapeDtypeStruct((t, D_MODEL), F32),
        scratch_shapes=[pltpu.VMEM((EXP_BLK, EXP_TOK), F32),
                        pltpu.VMEM((EXP_BLK, EXP_TOK), BF16),
                        pltpu.VMEM((D_MODEL, EXP_TOK), F32)],
        compiler_params=_cparams(("parallel", "arbitrary"), 58),
        name="experts",
    )(h2, u_b, vt_b, c1, p1, r2, p2, x1, mod3)


def _layer(x, c, ctx, c_ctx, w_ada, b_ada, norm1_w, norm2_w, w_in, q_norm_w, k_norm_w, na_rpb,
           conv_w, conv_b, lru_w_a, lru_b_a, lru_w_x, lru_b_x, lru_lambda, w_o_attn, w_o_lru,
           w_out, peer_w_q, peer_keys, peer_u, peer_v):
    batch, seq, d = x.shape
    n_ctx = ctx.shape[1]
    assert d == D_MODEL and n_ctx == CTX_LEN and seq % (ATT_QROWS * GRID_W) == 0 and batch < MOD_ROWS
    x2 = x.reshape(batch * seq, d)
    ctx2 = ctx.reshape(batch * n_ctx, d)

    c_rows = jnp.zeros((MOD_ROWS, d), F32).at[:batch].set(c).at[batch].set(c_ctx)
    mod3 = _mod(c_rows, w_ada, b_ada).reshape(MOD_ROWS, 1, 6 * d)

    w_in_b = w_in.astype(BF16)
    n1 = norm1_w.reshape(1, d)
    qn_row = (jnp.tile(q_norm_w, NA_HEADS) * (NA_HEAD_DIM ** -0.5)).reshape(1, NA_WIDTH)
    kn_row = jnp.tile(k_norm_w, NA_HEADS).reshape(1, NA_WIDTH)
    qkv, xr, gy, sga, sgb = _inproj(x2, mod3, n1, w_in_b, qn_row, kn_row, seq)
    kv_c, xr_c = _inproj_ctx(ctx2, mod3, n1, w_in_b, kn_row, batch)

    o_a = _attention(qkv, kv_c, _attn_bias_table(na_rpb), batch, seq)

    def gate_params(direction):
        w_gate = jnp.concatenate([lru_w_a[direction], lru_w_x[direction]], axis=-1).astype(BF16)
        b_gate = jnp.stack([lru_b_a[direction], lru_b_x[direction]])
        return w_gate, b_gate, lru_lambda[direction].reshape(1, LRU_WIDTH)

    h_bwd, u, u_c = _lru_pass(xr, xr_c, *gate_params(1), batch, seq, True,
                              conv=(conv_w, conv_b.reshape(1, LRU_WIDTH)))
    o_b = _lru_pass(u, u_c, *gate_params(0), batch, seq, False, h_bwd=h_bwd, gy=gy)

    keys = peer_keys.reshape(2 * PEER_HEADS, PEER_NKEYS, PEER_HALF).astype(BF16)
    x1, h2, st = _merge(o_a, o_b, sga, sgb, x2, mod3, w_o_attn.astype(BF16), w_o_lru.astype(BF16),
                        w_out.astype(BF16), norm2_w.reshape(1, d), peer_w_q.astype(BF16), keys, seq)
    c1, p1, r2, p2 = _select(st)
    out = _experts(h2, peer_u.astype(BF16), peer_v.astype(BF16).T, c1, p1, r2, p2, x1, mod3, seq)
    return out.reshape(batch, seq, d)


def kernel(x, c, ctx, c_ctx, w_ada, b_ada, norm1_w, norm2_w, w_in, q_norm_w, k_norm_w, na_rpb, conv_w, conv_b, lru_w_a, lru_b_a, lru_w_x, lru_b_x, lru_lambda, w_o_attn, w_o_lru, w_out, peer_w_q, peer_keys, peer_u, peer_v):
    depth = w_ada.shape[0]
    for layer in range(depth):
        x = _layer(x, c, ctx, c_ctx, w_ada[layer], b_ada[layer], norm1_w[layer], norm2_w[layer],
                   w_in[layer], q_norm_w[layer], k_norm_w[layer], na_rpb[layer], conv_w[layer],
                   conv_b[layer], lru_w_a[layer], lru_b_a[layer], lru_w_x[layer], lru_b_x[layer],
                   lru_lambda[layer], w_o_attn[layer], w_o_lru[layer], w_out[layer], peer_w_q[layer],
                   peer_keys[layer], peer_u[layer], peer_v[layer])
    return x
```
